```python
import math
import jax, jax.numpy as jnp
from jax import lax
import numpy as np

D_MODEL = 1024
BATCH = 8
SEQ = 4096
DEPTH = 2

CHUNK = 64
MIX_WIDTH = D_MODEL
ATT_HEADS = 8
HEAD_DIM = 64
ATT_WIDTH = ATT_HEADS * HEAD_DIM
CONV_CH = MIX_WIDTH - ATT_WIDTH
CONV_WIDTH = 31
D_FF = 4 * D_MODEL
QBLK = 128
N_IN = 3 * ATT_WIDTH + ATT_HEADS + 2 * CONV_CH
EPS = 1e-6

kernel_name = "fox_conformer_hybrid_trunk"


def rms_norm(x, g):
    xf = x.astype(jnp.float32)
    y = xf * lax.rsqrt(jnp.mean(xf * xf, axis=-1, keepdims=True) + EPS)
    return (y * g.astype(jnp.float32)).astype(x.dtype)


def layer_norm(x, g, b):
    xf = x.astype(jnp.float32)
    mu = jnp.mean(xf, axis=-1, keepdims=True)
    var = jnp.mean(jnp.square(xf - mu), axis=-1, keepdims=True)
    y = (xf - mu) * lax.rsqrt(var + EPS)
    return (y * g.astype(jnp.float32) + b.astype(jnp.float32)).astype(x.dtype)


def forgetting_attention(q, k, v, logf):
    S = q.shape[2]
    scale = 1.0 / math.sqrt(q.shape[-1])
    c = jnp.cumsum(logf.astype(jnp.float32), axis=-1)
    outs = []
    for blk in range(S // QBLK):
        q0, q1 = blk * QBLK, (blk + 1) * QBLK
        qb = q[:, :, q0:q1]
        kb = k[:, :, :q1]
        vb = v[:, :, :q1]
        s = (jnp.einsum('bhqd,bhkd->bhqk', qb, kb).astype(jnp.float32) * scale
             + (c[:, :, q0:q1, None] - c[:, :, None, :q1]))
        mask = jnp.arange(q0, q1)[:, None] >= jnp.arange(q1)[None, :]
        s = jnp.where(mask, s, -jnp.inf)
        p = jax.nn.softmax(s, axis=-1)
        outs.append(jnp.einsum('bhqk,bhkd->bhqd', p.astype(vb.dtype), vb))
    return jnp.concatenate(outs, axis=2)


def causal_depthwise_conv(x, w, b):
    W, C = w.shape
    y = lax.conv_general_dilated(
        x, w.reshape(W, 1, C).astype(x.dtype), window_strides=(1,), padding=[(W - 1, 0)],
        dimension_numbers=('NWC', 'WIO', 'NWC'), feature_group_count=C)
    return y + b.astype(x.dtype)


def hybrid_mixer(x, norm_g, w_in, b_f, q_norm_g, k_norm_g, conv_w, conv_b, conv_ln_g, conv_ln_b, w_o):
    B, S, _ = x.shape
    u = rms_norm(x, norm_g)
    proj = jnp.einsum('bsd,dn->bsn', u, w_in)
    o1 = ATT_WIDTH; o2 = 2 * ATT_WIDTH; o3 = 3 * ATT_WIDTH; o4 = o3 + ATT_HEADS
    q = proj[..., :o1].reshape(B, S, ATT_HEADS, HEAD_DIM)
    k = proj[..., o1:o2].reshape(B, S, ATT_HEADS, HEAD_DIM)
    v = proj[..., o2:o3].reshape(B, S, ATT_HEADS, HEAD_DIM)
    f_logit = proj[..., o3:o4]
    glu_in = proj[..., o4:]

    q = rms_norm(q, q_norm_g).transpose(0, 2, 1, 3)
    k = rms_norm(k, k_norm_g).transpose(0, 2, 1, 3)
    v = v.transpose(0, 2, 1, 3)
    logf = jax.nn.log_sigmoid(f_logit.astype(jnp.float32) + b_f.astype(jnp.float32)).transpose(0, 2, 1)
    att = forgetting_attention(q, k, v, logf).transpose(0, 2, 1, 3).reshape(B, S, ATT_WIDTH)

    a, g = jnp.split(glu_in, 2, axis=-1)
    h = a * jax.nn.sigmoid(g)
    h = causal_depthwise_conv(h, conv_w, conv_b)
    h = layer_norm(h, conv_ln_g, conv_ln_b)
    h = jax.nn.silu(h)

    mixed = jnp.concatenate([att, h.astype(att.dtype)], axis=-1)
    return x + jnp.einsum('bsm,md->bsd', mixed, w_o)


def sq_relu_mlp(x, norm_g, w1, w2):
    u = rms_norm(x, norm_g)
    h = jnp.square(jax.nn.relu(jnp.einsum('bsd,df->bsf', u, w1)))
    return x + jnp.einsum('bsf,fd->bsd', h, w2)


def setup_inputs(seed: int = 0) -> dict:
    key = jax.random.key(seed)
    ks = jax.random.split(key, 16)
    f32 = jnp.float32
    nrm = lambda k, shape, s: jax.random.normal(k, shape, f32) * s
    return {
        "x": jax.random.normal(ks[0], (BATCH, SEQ, D_MODEL), f32),
        "norm1_g": 1.0 + nrm(ks[1], (DEPTH, D_MODEL), 0.02),
        "w_in": nrm(ks[2], (DEPTH, D_MODEL, N_IN), D_MODEL ** -0.5),
        "b_f": 3.0 + nrm(ks[3], (DEPTH, ATT_HEADS), 0.1),
        "q_norm_g": 1.0 + nrm(ks[4], (DEPTH, HEAD_DIM), 0.02),
        "k_norm_g": 1.0 + nrm(ks[5], (DEPTH, HEAD_DIM), 0.02),
        "conv_w": nrm(ks[6], (DEPTH, CONV_WIDTH, CONV_CH), CONV_WIDTH ** -0.5),
        "conv_b": nrm(ks[7], (DEPTH, CONV_CH), 0.01),
        "conv_ln_g": 1.0 + nrm(ks[8], (DEPTH, CONV_CH), 0.02),
        "conv_ln_b": nrm(ks[9], (DEPTH, CONV_CH), 0.01),
        "w_o": nrm(ks[10], (DEPTH, MIX_WIDTH, D_MODEL), MIX_WIDTH ** -0.5),
        "norm2_g": 1.0 + nrm(ks[11], (DEPTH, D_MODEL), 0.02),
        "w_mlp_in": nrm(ks[12], (DEPTH, D_MODEL, D_FF), D_MODEL ** -0.5),
        "w_mlp_out": nrm(ks[13], (DEPTH, D_FF, D_MODEL), D_FF ** -0.5),
    }


def reference(x, norm1_g, w_in, b_f, q_norm_g, k_norm_g, conv_w, conv_b, conv_ln_g, conv_ln_b,
              w_o, norm2_g, w_mlp_in, w_mlp_out):
    for l in range(DEPTH):
        x = hybrid_mixer(x, norm1_g[l], w_in[l], b_f[l], q_norm_g[l], k_norm_g[l], conv_w[l],
                         conv_b[l], conv_ln_g[l], conv_ln_b[l], w_o[l])
        x = sq_relu_mlp(x, norm2_g[l], w_mlp_in[l], w_mlp_out[l])
    return x
```

```python
import functools
import math

import jax
import jax.numpy as jnp
from jax import lax
from jax.experimental import pallas as pl
from jax.experimental.pallas import tpu as pltpu

D_MODEL = 1024
ATT_HEADS = 8
HEAD_DIM = 64
ATT_WIDTH = ATT_HEADS * HEAD_DIM
CONV_CH = D_MODEL - ATT_WIDTH
CONV_WIDTH = 31
D_FF = 4 * D_MODEL
EPS = 1e-6
LOG2E = 1.4426950408889634

LANES = 128
MXU_TILE = 256
F_PAD = LANES
HALO = 32
NEG_BIG = -1e30

TM_IN = 512
TM_OUT = 512
TM_MLP = 512
FF_CHUNK = 512
TQ = 256
TK = 256
TS_CONV = 256
CONV_ROWS = 32
CUM_BLK = 256
VMEM_LIMIT = 56 * 1024 * 1024

bf16 = jnp.bfloat16
f32 = jnp.float32


def _dot(a, b):
    return jnp.dot(a, b, preferred_element_type=f32)


def _split2(y):
    hi = y.astype(bf16)
    lo = (y - hi.astype(f32)).astype(bf16)
    return hi, lo


def _split3(y):
    hi = y.astype(bf16)
    r = y - hi.astype(f32)
    mid = r.astype(bf16)
    lo = (r - mid.astype(f32)).astype(bf16)
    return hi, mid, lo


def _in_proj_kernel(x_ref, g1_ref, wq_ref, wk_ref, wv_ref, wa_ref, wg_ref, wf_ref, bf_ref,
                    gq_ref, gk_ref, gsum_ref, q_ref, k_ref, v_ref, h_ref, lf_ref):
    x = x_ref[...]
    ms = jnp.mean(x * x, axis=-1, keepdims=True)
    u = (x * lax.rsqrt(ms + EPS) * g1_ref[...]).astype(bf16)
    gsum = gsum_ref[...]

    def head_norm(y, gain):
        hi, lo = _split2(y * y)
        parts = []
        for c in range(ATT_WIDTH // MXU_TILE):
            sl = slice(c * MXU_TILE, (c + 1) * MXU_TILE)
            parts.append(_dot(hi[:, sl], gsum) + _dot(lo[:, sl], gsum))
        ss = jnp.concatenate(parts, axis=-1)
        return y * lax.rsqrt(ss * (1.0 / HEAD_DIM) + EPS) * gain

    q_ref[...] = head_norm(_dot(u, wq_ref[...]), gq_ref[...]).astype(bf16)
    k_ref[...] = head_norm(_dot(u, wk_ref[...]), gk_ref[...]).astype(bf16)
    v_ref[...] = _dot(u, wv_ref[...]).astype(bf16)
    a = _dot(u, wa_ref[...])
    g = _dot(u, wg_ref[...])
    h_ref[...] = a * jax.nn.sigmoid(g)
    lf_ref[...] = jax.nn.log_sigmoid(_dot(u, wf_ref[...]) + bf_ref[...])


def _in_proj(x2, g1, wq, wk, wv, wa, wg, wf, bfp, gq, gk, gsum):
    t = x2.shape[0]
    row = lambda n: pl.BlockSpec((TM_IN, n), lambda i: (i, 0))
    full = lambda a: pl.BlockSpec(a.shape, lambda i: (0,) * a.ndim)
    return pl.pallas_call(
        _in_proj_kernel,
        grid=(t // TM_IN,),
        in_specs=[row(D_MODEL)] + [full(a) for a in (g1, wq, wk, wv, wa, wg, wf, bfp, gq, gk, gsum)],
        out_specs=[row(ATT_WIDTH), row(ATT_WIDTH), row(ATT_WIDTH), row(CONV_CH), row(F_PAD)],
        out_shape=[jax.ShapeDtypeStruct((t, ATT_WIDTH), bf16)] * 3
        + [jax.ShapeDtypeStruct((t, CONV_CH), f32), jax.ShapeDtypeStruct((t, F_PAD), f32)],
        compiler_params=pltpu.CompilerParams(dimension_semantics=("parallel",), vmem_limit_bytes=VMEM_LIMIT),
        name="in_proj",
    )(x2, g1, wq, wk, wv, wa, wg, wf, bfp, gq, gk, gsum)


def _cumsum_kernel(lf_ref, tri_ref, c_ref):
    s = lf_ref.shape[1]
    tri = tri_ref[...]
    carry = jnp.zeros((ATT_HEADS, 1), f32)
    for j in range(s // CUM_BLK):
        xt = lf_ref[0, j * CUM_BLK:(j + 1) * CUM_BLK, :].T[:ATT_HEADS, :]
        hi, mid, lo = _split3(xt)
        cs = _dot(hi, tri) + _dot(mid, tri) + _dot(lo, tri) + carry
        c_ref[0, :, j * CUM_BLK:(j + 1) * CUM_BLK] = cs * (-LOG2E)
        carry = cs[:, CUM_BLK - 1:CUM_BLK]


def _cumsum(lf3, tri):
    b, s, _ = lf3.shape
    return pl.pallas_call(
        _cumsum_kernel,
        grid=(b,),
        in_specs=[pl.BlockSpec((1, s, F_PAD), lambda i: (i, 0, 0)),
                  pl.BlockSpec(tri.shape, lambda i: (0, 0))],
        out_specs=pl.BlockSpec((1, ATT_HEADS, s), lambda i: (i, 0, 0)),
        out_shape=jax.ShapeDtypeStruct((b, ATT_HEADS, s), f32),
        compiler_params=pltpu.CompilerParams(dimension_semantics=("parallel",)),
        name="forget_cumsum",
    )(lf3, tri)


def _attn_kernel(q_ref, k_ref, v_ref, c_ref, o_ref, m_ref, l_ref, acc_ref):
    qi = pl.program_id(2)
    q = q_ref[0]
    lane = lax.broadcasted_iota(jnp.int32, q.shape, 1)
    zero = jnp.zeros_like(q)
    qh = (jnp.where(lane < HEAD_DIM, q, zero), jnp.where(lane >= HEAD_DIM, q, zero))
    m_ref[...] = jnp.full(m_ref.shape, NEG_BIG, f32)
    l_ref[...] = jnp.zeros(l_ref.shape, f32)
    acc_ref[...] = jnp.zeros(acc_ref.shape, f32)

    def step(j, masked):
        start = pl.multiple_of(j * TK, TK)
        kb = k_ref[0, pl.ds(start, TK), :]
        vb = v_ref[0, pl.ds(start, TK), :]
        cb = c_ref[0, 0, :, pl.ds(start, TK)]
        for h in range(2):
            s = lax.dot_general(qh[h], kb, (((1,), (1,)), ((), ())), preferred_element_type=f32)
            s = s + cb[h:h + 1, :]
            if masked:
                r = lax.broadcasted_iota(jnp.int32, s.shape, 0)
                c = lax.broadcasted_iota(jnp.int32, s.shape, 1)
                s = jnp.where(r >= c, s, NEG_BIG)
            m_old = m_ref[h]
            m_new = jnp.maximum(m_old, jnp.max(s, axis=-1, keepdims=True))
            alpha = jnp.exp2(m_old - m_new)
            p = jnp.exp2(s - m_new)
            l_ref[h] = alpha * l_ref[h] + jnp.sum(p, axis=-1, keepdims=True)
            acc_ref[h] = alpha * acc_ref[h] + _dot(p.astype(bf16), vb)
            m_ref[h] = m_new

    def body(j, carry):
        step(j, False)
        return carry

    lax.fori_loop(0, qi, body, 0)
    step(qi, True)
    o0 = acc_ref[0] / l_ref[0]
    o1 = acc_ref[1] / l_ref[1]
    o_ref[0] = jnp.where(lane < HEAD_DIM, o0, o1).astype(o_ref.dtype)


def _attention(q3, k3, v3, c4):
    b, s, _ = q3.shape
    hp = ATT_WIDTH // LANES
    return pl.pallas_call(
        _attn_kernel,
        grid=(b, hp, s // TQ),
        in_specs=[pl.BlockSpec((1, TQ, LANES), lambda bi, hi, qi: (bi, qi, hi)),
                  pl.BlockSpec((1, s, LANES), lambda bi, hi, qi: (bi, 0, hi)),
                  pl.BlockSpec((1, s, LANES), lambda bi, hi, qi: (bi, 0, hi)),
                  pl.BlockSpec((1, 1, 2, s), lambda bi, hi, qi: (bi, hi, 0, 0))],
        out_specs=pl.BlockSpec((1, TQ, LANES), lambda bi, hi, qi: (bi, qi, hi)),
        out_shape=jax.ShapeDtypeStruct((b, s, ATT_WIDTH), bf16),
        scratch_shapes=[pltpu.VMEM((2, TQ, 1), f32), pltpu.VMEM((2, TQ, 1), f32),
                        pltpu.VMEM((2, TQ, LANES), f32)],
        compiler_params=pltpu.CompilerParams(dimension_semantics=("parallel", "parallel", "arbitrary")),
        name="fox_attention",
    )(q3, k3, v3, c4)


def _conv_kernel(hm_ref, hh_ref, cw_ref, cb_ref, lg_ref, lb_ref, o_ref, win_ref):
    i = pl.program_id(1)
    halo = hh_ref[0]
    win_ref[0:HALO, :] = jnp.where(i > 0, halo, jnp.zeros_like(halo))
    win_ref[HALO:HALO + TS_CONV, :] = hm_ref[0]
    off = HALO - (CONV_WIDTH - 1)
    for c in range(TS_CONV // CONV_ROWS):
        base = c * CONV_ROWS + off
        acc = win_ref[base:base + CONV_ROWS, :] * cw_ref[0:1, :]
        for w in range(1, CONV_WIDTH):
            acc = acc + win_ref[base + w:base + w + CONV_ROWS, :] * cw_ref[w:w + 1, :]
        y = acc + cb_ref[...]
        mu = jnp.mean(y, axis=-1, keepdims=True)
        d = y - mu
        var = jnp.mean(d * d, axis=-1, keepdims=True)
        z = d * lax.rsqrt(var + EPS) * lg_ref[...] + lb_ref[...]
        o_ref[0, c * CONV_ROWS:(c + 1) * CONV_ROWS, :] = (z * jax.nn.sigmoid(z)).astype(o_ref.dtype)


def _conv_module(h3, cw, cb, lg, lb):
    b, s, c = h3.shape
    ratio = TS_CONV // HALO
    full = lambda a: pl.BlockSpec(a.shape, lambda bi, i: (0,) * a.ndim)
    return pl.pallas_call(
        _conv_kernel,
        grid=(b, s // TS_CONV),
        in_specs=[pl.BlockSpec((1, TS_CONV, c), lambda bi, i: (bi, i, 0)),
                  pl.BlockSpec((1, HALO, c), lambda bi, i: (bi, jnp.maximum(i * ratio - 1, 0), 0)),
                  full(cw), full(cb), full(lg), full(lb)],
        out_specs=pl.BlockSpec((1, TS_CONV, c), lambda bi, i: (bi, i, 0)),
        out_shape=jax.ShapeDtypeStruct((b, s, c), bf16),
        scratch_shapes=[pltpu.VMEM((HALO + TS_CONV, c), f32)],
        compiler_params=pltpu.CompilerParams(dimension_semantics=("parallel", "parallel")),
        name="conv_module",
    )(h3, h3, cw, cb, lg, lb)


def _out_proj_kernel(x_ref, att_ref, hc_ref, woa_ref, woc_ref, o_ref):
    o_ref[...] = x_ref[...] + _dot(att_ref[...], woa_ref[...]) + _dot(hc_ref[...], woc_ref[...])


def _out_proj(x2, att2, hc2, woa, woc):
    t = x2.shape[0]
    row = lambda n: pl.BlockSpec((TM_OUT, n), lambda i: (i, 0))
    full = lambda a: pl.BlockSpec(a.shape, lambda i: (0,) * a.ndim)
    return pl.pallas_call(
        _out_proj_kernel,
        grid=(t // TM_OUT,),
        in_specs=[row(D_MODEL), row(ATT_WIDTH), row(CONV_CH), full(woa), full(woc)],
        out_specs=row(D_MODEL),
        out_shape=jax.ShapeDtypeStruct((t, D_MODEL), f32),
        compiler_params=pltpu.CompilerParams(dimension_semantics=("parallel",), vmem_limit_bytes=VMEM_LIMIT),
        name="out_proj",
    )(x2, att2, hc2, woa, woc)


def _mlp_kernel(x_ref, g2_ref, w1_ref, w2_ref, o_ref, acc_ref):
    x = x_ref[...]
    ms = jnp.mean(x * x, axis=-1, keepdims=True)
    u = (x * lax.rsqrt(ms + EPS) * g2_ref[...]).astype(bf16)
    for c in range(D_FF // FF_CHUNK):
        sl = slice(c * FF_CHUNK, (c + 1) * FF_CHUNK)
        h = jnp.maximum(_dot(u, w1_ref[:, sl]), 0.0)
        part = _dot((h * h).astype(bf16), w2_ref[sl, :])
        if c == 0:
            acc_ref[...] = part
        else:
            acc_ref[...] += part
    o_ref[...] = x + acc_ref[...]


def _mlp(x2, g2, w1, w2):
    t = x2.shape[0]
    row = pl.BlockSpec((TM_MLP, D_MODEL), lambda i: (i, 0))
    const = lambda a: pl.BlockSpec(a.shape, lambda i: (0,) * a.ndim, pipeline_mode=pl.Buffered(1))
    return pl.pallas_call(
        _mlp_kernel,
        grid=(t // TM_MLP,),
        in_specs=[row, const(g2), const(w1), const(w2)],
        out_specs=row,
        out_shape=jax.ShapeDtypeStruct((t, D_MODEL), f32),
        scratch_shapes=[pltpu.VMEM((TM_MLP, D_MODEL), f32)],
        compiler_params=pltpu.CompilerParams(dimension_semantics=("parallel",), vmem_limit_bytes=VMEM_LIMIT),
        name="sq_relu_mlp",
    )(x2, g2, w1, w2)


def kernel(x, norm1_g, w_in, b_f, q_norm_g, k_norm_g, conv_w, conv_b, conv_ln_g, conv_ln_b, w_o,
           norm2_g, w_mlp_in, w_mlp_out):
    b, s, d = x.shape
    depth = w_in.shape[0]
    o1, o2, o3 = ATT_WIDTH, 2 * ATT_WIDTH, 3 * ATT_WIDTH
    o4 = o3 + ATT_HEADS
    o5 = o4 + CONV_CH

    idx = jnp.arange(MXU_TILE)
    gsum = (idx[:, None] // HEAD_DIM == idx[None, :] // HEAD_DIM).astype(bf16)
    tri = (jnp.arange(CUM_BLK)[:, None] <= jnp.arange(CUM_BLK)[None, :]).astype(bf16)
    q_scale = LOG2E / math.sqrt(HEAD_DIM)

    x2 = x.reshape(b * s, d)
    for l in range(depth):
        wl = w_in[l]
        wq, wk, wv = (wl[:, :o1].astype(bf16), wl[:, o1:o2].astype(bf16), wl[:, o2:o3].astype(bf16))
        wf = jnp.pad(wl[:, o3:o4], ((0, 0), (0, F_PAD - ATT_HEADS))).astype(bf16)
        wa, wg = wl[:, o4:o5].astype(bf16), wl[:, o5:].astype(bf16)
        bfp = jnp.pad(b_f[l], (0, F_PAD - ATT_HEADS)).reshape(1, F_PAD)
        gq = (jnp.tile(q_norm_g[l], ATT_HEADS) * q_scale).reshape(1, ATT_WIDTH)
        gk = jnp.tile(k_norm_g[l], ATT_HEADS).reshape(1, ATT_WIDTH)

        q2, k2, v2, h2, lf2 = _in_proj(x2, norm1_g[l].reshape(1, d), wq, wk, wv, wa, wg, wf, bfp, gq, gk, gsum)
        c3 = _cumsum(lf2.reshape(b, s, F_PAD), tri)
        att = _attention(q2.reshape(b, s, ATT_WIDTH), k2.reshape(b, s, ATT_WIDTH),
                         v2.reshape(b, s, ATT_WIDTH), c3.reshape(b, ATT_HEADS // 2, 2, s))
        hc = _conv_module(h2.reshape(b, s, CONV_CH), conv_w[l], conv_b[l].reshape(1, CONV_CH),
                          conv_ln_g[l].reshape(1, CONV_CH), conv_ln_b[l].reshape(1, CONV_CH))
        wo = w_o[l].astype(bf16)
        x2 = _out_proj(x2, att.reshape(b * s, ATT_WIDTH), hc.reshape(b * s, CONV_CH),
                       wo[:ATT_WIDTH], wo[ATT_WIDTH:])
        x2 = _mlp(x2, norm2_g[l].reshape(1, d), w_mlp_in[l].astype(bf16), w_mlp_out[l].astype(bf16))
    return x2.reshape(b, s, d)
```

```python
import functools
import math

import jax
import jax.numpy as jnp
from jax import lax
from jax.experimental import pallas as pl
from jax.experimental.pallas import tpu as pltpu

D_MODEL = 1024
ATT_HEADS = 8
HEAD_DIM = 64
ATT_WIDTH = ATT_HEADS * HEAD_DIM
CONV_CH = D_MODEL - ATT_WIDTH
CONV_WIDTH = 31
D_FF = 4 * D_MODEL
EPS = 1e-6
LOG2E = 1.4426950408889634

LANES = 128
MXU_TILE = 256
F_PAD = LANES
HALO = 32
NEG_BIG = -1e30

TM_IN = 512
TM_OUT = 512
TM_MLP = 512
FF_CHUNK = 512
TQ = 256
TK = 256
TS_CONV = 256
CONV_ROWS = 32
CUM_BLK = 256
VMEM_LIMIT = 56 * 1024 * 1024

bf16 = jnp.bfloat16
f32 = jnp.float32


def _dot(a, b):
    return jnp.dot(a, b, preferred_element_type=f32)


def _split2(y):
    hi = y.astype(bf16)
    lo = (y - hi.astype(f32)).astype(bf16)
    return hi, lo


def _split3(y):
    hi = y.astype(bf16)
    r = y - hi.astype(f32)
    mid = r.astype(bf16)
    lo = (r - mid.astype(f32)).astype(bf16)
    return hi, mid, lo


def _in_proj_kernel(x_ref, g1_ref, wq_ref, wk_ref, wv_ref, wa_ref, wg_ref, wf_ref, bf_ref,
                    gq_ref, gk_ref, gsum_ref, q_ref, k_ref, vt_ref, h_ref, lf_ref):
    x = x_ref[0]
    ms = jnp.mean(x * x, axis=-1, keepdims=True)
    u = (x * lax.rsqrt(ms + EPS) * g1_ref[...]).astype(bf16)
    gsum = gsum_ref[...]

    def head_norm(y, gain):
        hi, lo = _split2(y * y)
        parts = []
        for c in range(ATT_WIDTH // MXU_TILE):
            sl = slice(c * MXU_TILE, (c + 1) * MXU_TILE)
            parts.append(_dot(hi[:, sl], gsum) + _dot(lo[:, sl], gsum))
        ss = jnp.concatenate(parts, axis=-1)
        return y * lax.rsqrt(ss * (1.0 / HEAD_DIM) + EPS) * gain

    q_ref[0] = head_norm(_dot(u, wq_ref[...]), gq_ref[...]).astype(bf16)
    k_ref[0] = head_norm(_dot(u, wk_ref[...]), gk_ref[...]).astype(bf16)
    vt_ref[0] = _dot(u, wv_ref[...]).T.astype(bf16)
    a = _dot(u, wa_ref[...])
    g = _dot(u, wg_ref[...])
    h_ref[0] = a * jax.nn.sigmoid(g)
    lf_ref[0] = jax.nn.log_sigmoid(_dot(u, wf_ref[...]) + bf_ref[...])


def _in_proj(x3, g1, wq, wk, wv, wa, wg, wf, bfp, gq, gk, gsum):
    b, s, _ = x3.shape
    row = lambda n: pl.BlockSpec((1, TM_IN, n), lambda bi, i: (bi, i, 0))
    full = lambda a: pl.BlockSpec(a.shape, lambda bi, i: (0,) * a.ndim)
    return pl.pallas_call(
        _in_proj_kernel,
        grid=(b, s // TM_IN),
        in_specs=[row(D_MODEL)] + [full(a) for a in (g1, wq, wk, wv, wa, wg, wf, bfp, gq, gk, gsum)],
        out_specs=[row(ATT_WIDTH), row(ATT_WIDTH),
                   pl.BlockSpec((1, ATT_WIDTH, TM_IN), lambda bi, i: (bi, 0, i)),
                   row(CONV_CH), row(F_PAD)],
        out_shape=[jax.ShapeDtypeStruct((b, s, ATT_WIDTH), bf16), jax.ShapeDtypeStruct((b, s, ATT_WIDTH), bf16),
                   jax.ShapeDtypeStruct((b, ATT_WIDTH, s), bf16),
                   jax.ShapeDtypeStruct((b, s, CONV_CH), f32), jax.ShapeDtypeStruct((b, s, F_PAD), f32)],
        compiler_params=pltpu.CompilerParams(dimension_semantics=("parallel", "parallel"),
                                             vmem_limit_bytes=VMEM_LIMIT),
        name="in_proj",
    )(x3, g1, wq, wk, wv, wa, wg, wf, bfp, gq, gk, gsum)


def _cumsum_kernel(lf_ref, tri_ref, sel_ref, cp_ref):
    s = lf_ref.shape[1]
    tri = tri_ref[...]
    sel = jnp.broadcast_to(sel_ref[...], (CUM_BLK, F_PAD))
    carry = jnp.zeros((1, F_PAD), f32)
    for j in range(s // CUM_BLK):
        rows = slice(j * CUM_BLK, (j + 1) * CUM_BLK)
        hi, mid, lo = _split3(lf_ref[0, rows, :])
        cs = _dot(tri, hi) + _dot(tri, mid) + _dot(tri, lo) + carry
        carry = cs[CUM_BLK - 1:CUM_BLK, :]
        bh, bm, bl = _split3(cs * (-LOG2E))
        piece = jnp.where(sel == 0, bh.astype(f32), jnp.where(sel == 1, bm.astype(f32), bl.astype(f32)))
        cp_ref[0, rows, :] = jnp.where(sel < 3, piece, 0.0).astype(bf16)


def _cumsum(lf3, tri, sel):
    b, s, _ = lf3.shape
    return pl.pallas_call(
        _cumsum_kernel,
        grid=(b,),
        in_specs=[pl.BlockSpec((1, s, F_PAD), lambda i: (i, 0, 0)),
                  pl.BlockSpec(tri.shape, lambda i: (0, 0)),
                  pl.BlockSpec(sel.shape, lambda i: (0, 0))],
        out_specs=pl.BlockSpec((1, s, F_PAD), lambda i: (i, 0, 0)),
        out_shape=jax.ShapeDtypeStruct((b, s, F_PAD), bf16),
        compiler_params=pltpu.CompilerParams(dimension_semantics=("parallel",)),
        name="forget_cumsum",
    )(lf3, tri, sel)


def _attn_kernel(q_ref, k_ref, cp_ref, vt_ref, o_ref, acc_ref, st_ref, p_ref):
    hp = pl.program_id(1)
    qi = pl.program_id(2)
    q = q_ref[0]
    lane = lax.broadcasted_iota(jnp.int32, q.shape, 1)
    qf = q.astype(f32)
    qct = []
    for h in range(2):
        g = 2 * hp + h
        qm = jnp.where((lane >= HEAD_DIM * h) & (lane < HEAD_DIM * (h + 1)), qf, 0.0)
        ones = jnp.where((lane >= 3 * g) & (lane < 3 * g + 3), 1.0, 0.0)
        qct.append(jnp.concatenate([qm.T, ones.T], axis=0).astype(bf16))
    rows = [slice(HEAD_DIM * h, HEAD_DIM * (h + 1)) for h in range(2)]

    def scores(j):
        start = pl.multiple_of(j * TK, TK)
        kc = jnp.concatenate([k_ref[0, pl.ds(start, TK), :], cp_ref[0, pl.ds(start, TK), :]], axis=1)
        return [_dot(kc, qct[h]) for h in range(2)]

    def weighted_values(j):
        start = pl.multiple_of(j * TK, TK)
        return [_dot(vt_ref[0, rows[h], pl.ds(start, TK)], p_ref[h]) for h in range(2)]

    def softmax_update(st, mx, m_old, l_old):
        m_new = jnp.maximum(m_old, mx)
        alpha = jnp.exp2(m_old - m_new)
        p = jnp.exp2(st - m_new)
        return m_new, alpha * l_old + jnp.sum(p, axis=0, keepdims=True), alpha, p.astype(bf16)

    acc_ref[...] = jnp.zeros(acc_ref.shape, f32)
    p_ref[...] = jnp.zeros(p_ref.shape, bf16)
    st0 = scores(0)
    init = []
    for h in range(2):
        st_ref[h] = st0[h]
        init += [jnp.full((1, TQ), NEG_BIG, f32), jnp.zeros((1, TQ), f32), jnp.ones((1, TQ), f32),
                 jnp.max(st0[h], axis=0, keepdims=True)]

    def body(j, carry):
        pv_prev = weighted_values(jnp.maximum(j - 1, 0))
        st_next = scores(j + 1)
        out = []
        for h in range(2):
            m_old, l_old, a_prev, mx = carry[4 * h:4 * h + 4]
            m_new, l_new, alpha, p = softmax_update(st_ref[h], mx, m_old, l_old)
            acc_ref[rows[h], :] = a_prev * acc_ref[rows[h], :] + pv_prev[h]
            p_ref[h] = p
            st_ref[h] = st_next[h]
            out += [m_new, l_new, alpha, jnp.max(st_next[h], axis=0, keepdims=True)]
        return tuple(out)

    carry = lax.fori_loop(0, qi, body, tuple(init))

    pv_prev = weighted_values(jnp.maximum(qi - 1, 0))
    start = pl.multiple_of(qi * TK, TK)
    o_t = []
    for h in range(2):
        m_old, l_old, a_prev, _ = carry[4 * h:4 * h + 4]
        st = st_ref[h]
        r = lax.broadcasted_iota(jnp.int32, st.shape, 0)
        c = lax.broadcasted_iota(jnp.int32, st.shape, 1)
        st = jnp.where(r <= c, st, NEG_BIG)
        _, l_new, alpha, p = softmax_update(st, jnp.max(st, axis=0, keepdims=True), m_old, l_old)
        acc = a_prev * acc_ref[rows[h], :] + pv_prev[h]
        acc = alpha * acc + _dot(vt_ref[0, rows[h], pl.ds(start, TK)], p)
        o_t.append(acc / l_new)
    o_ref[0] = jnp.concatenate(o_t, axis=0).T.astype(o_ref.dtype)


def _attention(q3, k3, cp3, vt3):
    b, s, _ = q3.shape
    hp = ATT_WIDTH // LANES
    return pl.pallas_call(
        _attn_kernel,
        grid=(b, hp, s // TQ),
        in_specs=[pl.BlockSpec((1, TQ, LANES), lambda bi, hi, qi: (bi, qi, hi)),
                  pl.BlockSpec((1, s, LANES), lambda bi, hi, qi: (bi, 0, hi)),
                  pl.BlockSpec((1, s, F_PAD), lambda bi, hi, qi: (bi, 0, 0)),
                  pl.BlockSpec((1, LANES, s), lambda bi, hi, qi: (bi, hi, 0))],
        out_specs=pl.BlockSpec((1, TQ, LANES), lambda bi, hi, qi: (bi, qi, hi)),
        out_shape=jax.ShapeDtypeStruct((b, s, ATT_WIDTH), bf16),
        scratch_shapes=[pltpu.VMEM((LANES, TQ), f32), pltpu.VMEM((2, TK, TQ), f32),
                        pltpu.VMEM((2, TK, TQ), bf16)],
        compiler_params=pltpu.CompilerParams(dimension_semantics=("parallel", "parallel", "arbitrary")),
        name="fox_attention",
    )(q3, k3, cp3, vt3)


def _conv_kernel(hm_ref, hh_ref, cw_ref, cb_ref, lg_ref, lb_ref, o_ref, win_ref):
    i = pl.program_id(1)
    halo = hh_ref[0]
    win_ref[0:HALO, :] = jnp.where(i > 0, halo, jnp.zeros_like(halo))
    win_ref[HALO:HALO + TS_CONV, :] = hm_ref[0]
    off = HALO - (CONV_WIDTH - 1)
    for c in range(TS_CONV // CONV_ROWS):
        base = c * CONV_ROWS + off
        acc = win_ref[base:base + CONV_ROWS, :] * cw_ref[0:1, :]
        for w in range(1, CONV_WIDTH):
            acc = acc + win_ref[base + w:base + w + CONV_ROWS, :] * cw_ref[w:w + 1, :]
        y = acc + cb_ref[...]
        mu = jnp.mean(y, axis=-1, keepdims=True)
        d = y - mu
        var = jnp.mean(d * d, axis=-1, keepdims=True)
        z = d * lax.rsqrt(var + EPS) * lg_ref[...] + lb_ref[...]
        o_ref[0, c * CONV_ROWS:(c + 1) * CONV_ROWS, :] = (z * jax.nn.sigmoid(z)).astype(o_ref.dtype)


def _conv_module(h3, cw, cb, lg, lb):
    b, s, c = h3.shape
    ratio = TS_CONV // HALO
    full = lambda a: pl.BlockSpec(a.shape, lambda bi, i: (0,) * a.ndim)
    return pl.pallas_call(
        _conv_kernel,
        grid=(b, s // TS_CONV),
        in_specs=[pl.BlockSpec((1, TS_CONV, c), lambda bi, i: (bi, i, 0)),
                  pl.BlockSpec((1, HALO, c), lambda bi, i: (bi, jnp.maximum(i * ratio - 1, 0), 0)),
                  full(cw), full(cb), full(lg), full(lb)],
        out_specs=pl.BlockSpec((1, TS_CONV, c), lambda bi, i: (bi, i, 0)),
        out_shape=jax.ShapeDtypeStruct((b, s, c), bf16),
        scratch_shapes=[pltpu.VMEM((HALO + TS_CONV, c), f32)],
        compiler_params=pltpu.CompilerParams(dimension_semantics=("parallel", "parallel")),
        name="conv_module",
    )(h3, h3, cw, cb, lg, lb)


def _out_proj_kernel(x_ref, att_ref, hc_ref, woa_ref, woc_ref, o_ref):
    o_ref[...] = x_ref[...] + _dot(att_ref[...], woa_ref[...]) + _dot(hc_ref[...], woc_ref[...])


def _out_proj(x2, att2, hc2, woa, woc):
    t = x2.shape[0]
    row = lambda n: pl.BlockSpec((TM_OUT, n), lambda i: (i, 0))
    full = lambda a: pl.BlockSpec(a.shape, lambda i: (0,) * a.ndim)
    return pl.pallas_call(
        _out_proj_kernel,
        grid=(t // TM_OUT,),
        in_specs=[row(D_MODEL), row(ATT_WIDTH), row(CONV_CH), full(woa), full(woc)],
        out_specs=row(D_MODEL),
        out_shape=jax.ShapeDtypeStruct((t, D_MODEL), f32),
        compiler_params=pltpu.CompilerParams(dimension_semantics=("parallel",), vmem_limit_bytes=VMEM_LIMIT),
        name="out_proj",
    )(x2, att2, hc2, woa, woc)


def _mlp_kernel(x_ref, g2_ref, w1_ref, w2_ref, o_ref, acc_ref):
    x = x_ref[...]
    ms = jnp.mean(x * x, axis=-1, keepdims=True)
    u = (x * lax.rsqrt(ms + EPS) * g2_ref[...]).astype(bf16)
    for c in range(D_FF // FF_CHUNK):
        sl = slice(c * FF_CHUNK, (c + 1) * FF_CHUNK)
        h = jnp.maximum(_dot(u, w1_ref[:, sl]), 0.0)
        part = _dot((h * h).astype(bf16), w2_ref[sl, :])
        if c == 0:
            acc_ref[...] = part
        else:
            acc_ref[...] += part
    o_ref[...] = x + acc_ref[...]


def _mlp(x2, g2, w1, w2):
    t = x2.shape[0]
    row = pl.BlockSpec((TM_MLP, D_MODEL), lambda i: (i, 0))
    const = lambda a: pl.BlockSpec(a.shape, lambda i: (0,) * a.ndim, pipeline_mode=pl.Buffered(1))
    return pl.pallas_call(
        _mlp_kernel,
        grid=(t // TM_MLP,),
        in_specs=[row, const(g2), const(w1), const(w2)],
        out_specs=row,
        out_shape=jax.ShapeDtypeStruct((t, D_MODEL), f32),
        scratch_shapes=[pltpu.VMEM((TM_MLP, D_MODEL), f32)],
        compiler_params=pltpu.CompilerParams(dimension_semantics=("parallel",), vmem_limit_bytes=VMEM_LIMIT),
        name="sq_relu_mlp",
    )(x2, g2, w1, w2)


def kernel(x, norm1_g, w_in, b_f, q_norm_g, k_norm_g, conv_w, conv_b, conv_ln_g, conv_ln_b, w_o,
           norm2_g, w_mlp_in, w_mlp_out):
    b, s, d = x.shape
    depth = w_in.shape[0]
    o1, o2, o3 = ATT_WIDTH, 2 * ATT_WIDTH, 3 * ATT_WIDTH
    o4 = o3 + ATT_HEADS
    o5 = o4 + CONV_CH

    idx = jnp.arange(MXU_TILE)
    gsum = (idx[:, None] // HEAD_DIM == idx[None, :] // HEAD_DIM).astype(bf16)
    tri = (jnp.arange(CUM_BLK)[:, None] >= jnp.arange(CUM_BLK)[None, :]).astype(bf16)
    lane_ids = jnp.arange(F_PAD, dtype=jnp.int32)
    sel = jnp.where(lane_ids < 3 * ATT_HEADS, lane_ids % 3, 3).reshape(1, F_PAD)
    q_scale = LOG2E / math.sqrt(HEAD_DIM)

    x2 = x.reshape(b * s, d)
    for l in range(depth):
        wl = w_in[l]
        wq, wk, wv = (wl[:, :o1].astype(bf16), wl[:, o1:o2].astype(bf16), wl[:, o2:o3].astype(bf16))
        wf = jnp.pad(jnp.repeat(wl[:, o3:o4], 3, axis=1), ((0, 0), (0, F_PAD - 3 * ATT_HEADS))).astype(bf16)
        wa, wg = wl[:, o4:o5].astype(bf16), wl[:, o5:].astype(bf16)
        bfp = jnp.pad(jnp.repeat(b_f[l], 3), (0, F_PAD - 3 * ATT_HEADS)).reshape(1, F_PAD)
        gq = (jnp.tile(q_norm_g[l], ATT_HEADS) * q_scale).reshape(1, ATT_WIDTH)
        gk = jnp.tile(k_norm_g[l], ATT_HEADS).reshape(1, ATT_WIDTH)

        q3, k3, vt3, h3, lf3 = _in_proj(x2.reshape(b, s, d), norm1_g[l].reshape(1, d), wq, wk, wv, wa, wg, wf,
                                        bfp, gq, gk, gsum)
        cp3 = _cumsum(lf3, tri, sel)
        att = _attention(q3, k3, cp3, vt3)
        hc = _conv_module(h3, conv_w[l], conv_b[l].reshape(1, CONV_CH),
                          conv_ln_g[l].reshape(1, CONV_CH), conv_ln_b[l].reshape(1, CONV_CH))
        wo = w_o[l].astype(bf16)
        x2 = _out_proj(x2, att.reshape(b * s, ATT_WIDTH), hc.reshape(b * s, CONV_CH),
                       wo[:ATT_WIDTH], wo[ATT_WIDTH:])
        x2 = _mlp(x2, norm2_g[l].reshape(1, d), w_mlp_in[l].astype(bf16), w_mlp_out[l].astype(bf16))
    return x2.reshape(b, s, d)
```

```python
import functools
import math

import jax
import jax.numpy as jnp
from jax import lax
from jax.experimental import pallas as pl
from jax.experimental.pallas import tpu as pltpu

D_MODEL = 1024
ATT_HEADS = 8
HEAD_DIM = 64
ATT_WIDTH = ATT_HEADS * HEAD_DIM
CONV_CH = D_MODEL - ATT_WIDTH
CONV_WIDTH = 31
D_FF = 4 * D_MODEL
EPS = 1e-6
LOG2E = 1.4426950408889634

LANES = 128
SUBLANES = 8
MXU_TILE = 256
F_PAD = LANES
HALO = 32
NEG_BIG = -1e30

TM_IN = 512
TM_OUT = 512
TM_MLP = 512
FF_CHUNK = 512
ATT_BLK = 512
TS_CONV = 256
CONV_ROWS = 32
CUM_BLK = 256
VMEM_LIMIT = 56 * 1024 * 1024

bf16 = jnp.bfloat16
f32 = jnp.float32


def _dot(a, b):
    return jnp.dot(a, b, preferred_element_type=f32)


def _split2(y):
    hi = y.astype(bf16)
    lo = (y - hi.astype(f32)).astype(bf16)
    return hi, lo


def _split3(y):
    hi = y.astype(bf16)
    r = y - hi.astype(f32)
    mid = r.astype(bf16)
    lo = (r - mid.astype(f32)).astype(bf16)
    return hi, mid, lo


def _in_proj_kernel(x_ref, g1_ref, wq_ref, wk_ref, wv_ref, wa_ref, wg_ref, wf_ref, bf_ref,
                    gq_ref, gk_ref, gsum_ref, qt_ref, k_ref, vt_ref, h_ref, lf_ref):
    x = x_ref[0]
    ms = jnp.mean(x * x, axis=-1, keepdims=True)
    u = (x * lax.rsqrt(ms + EPS) * g1_ref[...]).astype(bf16)
    gsum = gsum_ref[...]

    def head_norm(y, gain):
        hi, lo = _split2(y * y)
        parts = []
        for c in range(ATT_WIDTH // MXU_TILE):
            sl = slice(c * MXU_TILE, (c + 1) * MXU_TILE)
            parts.append(_dot(hi[:, sl], gsum) + _dot(lo[:, sl], gsum))
        ss = jnp.concatenate(parts, axis=-1)
        return y * lax.rsqrt(ss * (1.0 / HEAD_DIM) + EPS) * gain

    qt_ref[0] = head_norm(_dot(u, wq_ref[...]), gq_ref[...]).T.astype(bf16)
    k_ref[0] = head_norm(_dot(u, wk_ref[...]), gk_ref[...]).astype(bf16)
    vt_ref[0] = _dot(u, wv_ref[...]).T.astype(bf16)
    a = _dot(u, wa_ref[...])
    g = _dot(u, wg_ref[...])
    h_ref[0] = a * jax.nn.sigmoid(g)
    lf_ref[0] = jax.nn.log_sigmoid(_dot(u, wf_ref[...]) + bf_ref[...])


def _in_proj(x3, g1, wq, wk, wv, wa, wg, wf, bfp, gq, gk, gsum):
    b, s, _ = x3.shape
    row = lambda n: pl.BlockSpec((1, TM_IN, n), lambda bi, i: (bi, i, 0))
    col = pl.BlockSpec((1, ATT_WIDTH, TM_IN), lambda bi, i: (bi, 0, i))
    full = lambda a: pl.BlockSpec(a.shape, lambda bi, i: (0,) * a.ndim)
    return pl.pallas_call(
        _in_proj_kernel,
        grid=(b, s // TM_IN),
        in_specs=[row(D_MODEL)] + [full(a) for a in (g1, wq, wk, wv, wa, wg, wf, bfp, gq, gk, gsum)],
        out_specs=[col, row(ATT_WIDTH), col, row(CONV_CH), row(F_PAD)],
        out_shape=[jax.ShapeDtypeStruct((b, ATT_WIDTH, s), bf16), jax.ShapeDtypeStruct((b, s, ATT_WIDTH), bf16),
                   jax.ShapeDtypeStruct((b, ATT_WIDTH, s), bf16),
                   jax.ShapeDtypeStruct((b, s, CONV_CH), f32), jax.ShapeDtypeStruct((b, s, F_PAD), f32)],
        compiler_params=pltpu.CompilerParams(dimension_semantics=("parallel", "parallel"),
                                             vmem_limit_bytes=VMEM_LIMIT),
        name="in_proj",
    )(x3, g1, wq, wk, wv, wa, wg, wf, bfp, gq, gk, gsum)


def _cumsum_kernel(lf_ref, tri_ref, sel_ref, cp_ref):
    s = lf_ref.shape[1]
    tri = tri_ref[...]
    sel = jnp.broadcast_to(sel_ref[...], (CUM_BLK, F_PAD))
    carry = jnp.zeros((1, F_PAD), f32)
    for j in range(s // CUM_BLK):
        rows = slice(j * CUM_BLK, (j + 1) * CUM_BLK)
        hi, mid, lo = _split3(lf_ref[0, rows, :])
        cs = _dot(tri, hi) + _dot(tri, mid) + _dot(tri, lo) + carry
        carry = cs[CUM_BLK - 1:CUM_BLK, :]
        bh, bm, bl = _split3(cs * (-LOG2E))
        piece = jnp.where(sel == 0, bh.astype(f32), jnp.where(sel == 1, bm.astype(f32), bl.astype(f32)))
        cp_ref[0, rows, :] = jnp.where(sel < 3, piece, 0.0).astype(bf16)


def _cumsum(lf3, tri, sel):
    b, s, _ = lf3.shape
    return pl.pallas_call(
        _cumsum_kernel,
        grid=(b,),
        in_specs=[pl.BlockSpec((1, s, F_PAD), lambda i: (i, 0, 0)),
                  pl.BlockSpec(tri.shape, lambda i: (0, 0)),
                  pl.BlockSpec(sel.shape, lambda i: (0, 0))],
        out_specs=pl.BlockSpec((1, s, F_PAD), lambda i: (i, 0, 0)),
        out_shape=jax.ShapeDtypeStruct((b, s, F_PAD), bf16),
        compiler_params=pltpu.CompilerParams(dimension_semantics=("parallel",)),
        name="forget_cumsum",
    )(lf3, tri, sel)


def _attn_kernel(qt_ref, k_ref, cp_ref, vt_ref, mask_ref, o_ref, acc_ref, st_a, st_b, p_a, p_b):
    hp = pl.program_id(1)
    nblk = k_ref.shape[1] // ATT_BLK
    rows = [slice(HEAD_DIM * h, HEAD_DIM * (h + 1)) for h in range(2)]
    row_id = lax.broadcasted_iota(jnp.int32, (LANES, ATT_BLK), 0)
    zeros_t = jnp.zeros((HEAD_DIM, ATT_BLK), bf16)
    ones_t = [jnp.where((row_id >= 3 * (2 * hp + h)) & (row_id < 3 * (2 * hp + h) + 3), 1.0, 0.0).astype(bf16)
              for h in range(2)]

    def blk(i):
        return pl.ds(pl.multiple_of(i * ATT_BLK, ATT_BLK), ATT_BLK)

    def scores(qi, kj, st_ref):
        kc = jnp.concatenate([k_ref[0, blk(kj), :], cp_ref[0, blk(kj), :]], axis=1)
        qt = qt_ref[0, :, blk(qi)]
        mask = mask_ref[(qi == kj).astype(jnp.int32)]
        mx = []
        for h in range(2):
            parts = [qt[rows[0], :], zeros_t] if h == 0 else [zeros_t, qt[rows[1], :]]
            st = _dot(kc, jnp.concatenate(parts + [ones_t[h]], axis=0)) + mask
            st_ref[h] = st
            mx.append(jnp.max(st, axis=0, keepdims=True))
        return mx

    def advance(qi, kj):
        last = kj == qi
        qn = jnp.minimum(jnp.where(last, qi + 1, qi), nblk - 1)
        kn = jnp.minimum(jnp.where(last, 0, kj + 1), qn)
        return qn, kn

    def finish_prev(h, a_prev, pv_prev, l_prev, qi_prev):
        acc = a_prev * acc_ref[rows[h], :] + pv_prev
        acc_ref[rows[h], :] = acc
        return acc * (1.0 / l_prev)

    def half_step(carry, st_cur, st_next, p_cur, p_prev):
        qi, kj, qi_prev, kj_prev = carry[:4]
        pv_prev = [_dot(vt_ref[0, rows[h], blk(kj_prev)], p_prev[h]) for h in range(2)]
        qn, kn = advance(qi, kj)
        mx_next = scores(qn, kn, st_next)
        first = kj == 0
        out, o_t = [], []
        for h in range(2):
            m_old, l_old, a_prev, mx = carry[4 + 4 * h:8 + 4 * h]
            o_t.append(finish_prev(h, a_prev, pv_prev[h], l_old, qi_prev))
            m_old = jnp.where(first, NEG_BIG, m_old)
            l_old = jnp.where(first, 0.0, l_old)
            m_new = jnp.maximum(m_old, mx)
            alpha = jnp.exp2(m_old - m_new)
            p = jnp.exp2(st_cur[h] - m_new)
            p_cur[h] = p.astype(bf16)
            out += [m_new, alpha * l_old + jnp.sum(p, axis=0, keepdims=True), alpha, mx_next[h]]
        o_ref[0, blk(qi_prev), :] = jnp.concatenate(o_t, axis=0).T.astype(o_ref.dtype)
        return (qn, kn, qi, kj) + tuple(out)

    acc_ref[...] = jnp.zeros(acc_ref.shape, f32)
    p_b[...] = jnp.zeros(p_b.shape, bf16)
    zero = jnp.int32(0)
    mx0 = scores(zero, zero, st_a)
    init = (zero, zero, zero, zero)
    for h in range(2):
        init += (jnp.full((1, ATT_BLK), NEG_BIG, f32), jnp.ones((1, ATT_BLK), f32),
                 jnp.ones((1, ATT_BLK), f32), mx0[h])

    def body(_, carry):
        carry = half_step(carry, st_a, st_b, p_a, p_b)
        return half_step(carry, st_b, st_a, p_b, p_a)

    npairs = nblk * (nblk + 1) // 2
    carry = lax.fori_loop(0, npairs // 2, body, init)

    qi_last, kj_last = nblk - 1, nblk - 1
    o_t = []
    for h in range(2):
        _, l_fin, a_prev, _ = carry[4 + 4 * h:8 + 4 * h]
        pv = _dot(vt_ref[0, rows[h], blk(kj_last)], p_b[h])
        o_t.append(finish_prev(h, a_prev, pv, l_fin, qi_last))
    o_ref[0, blk(qi_last), :] = jnp.concatenate(o_t, axis=0).T.astype(o_ref.dtype)


def _attention(qt3, k3, cp3, vt3, mask):
    b, s, _ = k3.shape
    nblk = s // ATT_BLK
    assert (nblk * (nblk + 1) // 2) % 2 == 0, "pair loop is unrolled by two"
    hp = ATT_WIDTH // LANES
    return pl.pallas_call(
        _attn_kernel,
        grid=(b, hp),
        in_specs=[pl.BlockSpec((1, LANES, s), lambda bi, hi: (bi, hi, 0)),
                  pl.BlockSpec((1, s, LANES), lambda bi, hi: (bi, 0, hi)),
                  pl.BlockSpec((1, s, F_PAD), lambda bi, hi: (bi, 0, 0)),
                  pl.BlockSpec((1, LANES, s), lambda bi, hi: (bi, hi, 0)),
                  pl.BlockSpec(mask.shape, lambda bi, hi: (0, 0, 0))],
        out_specs=pl.BlockSpec((1, s, LANES), lambda bi, hi: (bi, 0, hi)),
        out_shape=jax.ShapeDtypeStruct((b, s, ATT_WIDTH), bf16),
        scratch_shapes=[pltpu.VMEM((LANES, ATT_BLK), f32),
                        pltpu.VMEM((2, ATT_BLK, ATT_BLK), f32), pltpu.VMEM((2, ATT_BLK, ATT_BLK), f32),
                        pltpu.VMEM((2, ATT_BLK, ATT_BLK), bf16), pltpu.VMEM((2, ATT_BLK, ATT_BLK), bf16)],
        compiler_params=pltpu.CompilerParams(dimension_semantics=("parallel", "parallel"),
                                             vmem_limit_bytes=VMEM_LIMIT),
        name="fox_attention",
    )(qt3, k3, cp3, vt3, mask)


def _conv_kernel(hm_ref, hh_ref, cw_ref, cb_ref, lg_ref, lb_ref, o_ref, win_ref, sh_ref):
    i = pl.program_id(1)
    halo = hh_ref[0]
    win_ref[0:HALO, :] = jnp.where(i > 0, halo, jnp.zeros_like(halo))
    win_ref[HALO:HALO + TS_CONV, :] = hm_ref[0]
    n_sh = sh_ref.shape[1]
    for r in range(1, SUBLANES):
        sh_ref[r] = win_ref[r:r + n_sh, :]
    off = HALO - (CONV_WIDTH - 1)
    for c in range(TS_CONV // CONV_ROWS):
        acc = None
        for w in range(CONV_WIDTH):
            r = (off + w) % SUBLANES
            base = c * CONV_ROWS + (off + w) - r
            src = win_ref if r == 0 else sh_ref.at[r]
            x = src[base:base + CONV_ROWS, :].reshape(CONV_ROWS // SUBLANES, SUBLANES, -1)
            term = (x * cw_ref[w]).reshape(CONV_ROWS, -1)
            acc = term if acc is None else acc + term
        y = acc + cb_ref[...]
        mu = jnp.mean(y, axis=-1, keepdims=True)
        d = y - mu
        var = jnp.mean(d * d, axis=-1, keepdims=True)
        z = d * lax.rsqrt(var + EPS) * lg_ref[...] + lb_ref[...]
        o_ref[0, c * CONV_ROWS:(c + 1) * CONV_ROWS, :] = (z * jax.nn.sigmoid(z)).astype(o_ref.dtype)


def _conv_module(h3, cw, cb, lg, lb):
    b, s, c = h3.shape
    ratio = TS_CONV // HALO
    full = lambda a: pl.BlockSpec(a.shape, lambda bi, i: (0,) * a.ndim)
    return pl.pallas_call(
        _conv_kernel,
        grid=(b, s // TS_CONV),
        in_specs=[pl.BlockSpec((1, TS_CONV, c), lambda bi, i: (bi, i, 0)),
                  pl.BlockSpec((1, HALO, c), lambda bi, i: (bi, jnp.maximum(i * ratio - 1, 0), 0)),
                  full(cw), full(cb), full(lg), full(lb)],
        out_specs=pl.BlockSpec((1, TS_CONV, c), lambda bi, i: (bi, i, 0)),
        out_shape=jax.ShapeDtypeStruct((b, s, c), bf16),
        scratch_shapes=[pltpu.VMEM((HALO + TS_CONV, c), f32),
                        pltpu.VMEM((SUBLANES, HALO + TS_CONV - SUBLANES, c), f32)],
        compiler_params=pltpu.CompilerParams(dimension_semantics=("parallel", "parallel")),
        name="conv_module",
    )(h3, h3, cw, cb, lg, lb)


def _out_proj_kernel(x_ref, att_ref, hc_ref, woa_ref, woc_ref, o_ref):
    o_ref[...] = x_ref[...] + _dot(att_ref[...], woa_ref[...]) + _dot(hc_ref[...], woc_ref[...])


def _out_proj(x2, att2, hc2, woa, woc):
    t = x2.shape[0]
    row = lambda n: pl.BlockSpec((TM_OUT, n), lambda i: (i, 0))
    full = lambda a: pl.BlockSpec(a.shape, lambda i: (0,) * a.ndim)
    return pl.pallas_call(
        _out_proj_kernel,
        grid=(t // TM_OUT,),
        in_specs=[row(D_MODEL), row(ATT_WIDTH), row(CONV_CH), full(woa), full(woc)],
        out_specs=row(D_MODEL),
        out_shape=jax.ShapeDtypeStruct((t, D_MODEL), f32),
        compiler_params=pltpu.CompilerParams(dimension_semantics=("parallel",), vmem_limit_bytes=VMEM_LIMIT),
        name="out_proj",
    )(x2, att2, hc2, woa, woc)


def _mlp_kernel(x_ref, g2_ref, w1_ref, w2_ref, o_ref, acc_ref):
    x = x_ref[...]
    ms = jnp.mean(x * x, axis=-1, keepdims=True)
    u = (x * lax.rsqrt(ms + EPS) * g2_ref[...]).astype(bf16)
    for c in range(D_FF // FF_CHUNK):
        sl = slice(c * FF_CHUNK, (c + 1) * FF_CHUNK)
        h = jnp.maximum(_dot(u, w1_ref[:, sl]), 0.0)
        part = _dot((h * h).astype(bf16), w2_ref[sl, :])
        if c == 0:
            acc_ref[...] = part
        else:
            acc_ref[...] += part
    o_ref[...] = x + acc_ref[...]


def _mlp(x2, g2, w1, w2):
    t = x2.shape[0]
    row = pl.BlockSpec((TM_MLP, D_MODEL), lambda i: (i, 0))
    const = lambda a: pl.BlockSpec(a.shape, lambda i: (0,) * a.ndim, pipeline_mode=pl.Buffered(1))
    return pl.pallas_call(
        _mlp_kernel,
        grid=(t // TM_MLP,),
        in_specs=[row, const(g2), const(w1), const(w2)],
        out_specs=row,
        out_shape=jax.ShapeDtypeStruct((t, D_MODEL), f32),
        scratch_shapes=[pltpu.VMEM((TM_MLP, D_MODEL), f32)],
        compiler_params=pltpu.CompilerParams(dimension_semantics=("parallel",), vmem_limit_bytes=VMEM_LIMIT),
        name="sq_relu_mlp",
    )(x2, g2, w1, w2)


def kernel(x, norm1_g, w_in, b_f, q_norm_g, k_norm_g, conv_w, conv_b, conv_ln_g, conv_ln_b, w_o,
           norm2_g, w_mlp_in, w_mlp_out):
    b, s, d = x.shape
    depth = w_in.shape[0]
    o1, o2, o3 = ATT_WIDTH, 2 * ATT_WIDTH, 3 * ATT_WIDTH
    o4 = o3 + ATT_HEADS
    o5 = o4 + CONV_CH

    idx = jnp.arange(MXU_TILE)
    gsum = (idx[:, None] // HEAD_DIM == idx[None, :] // HEAD_DIM).astype(bf16)
    tri = (jnp.arange(CUM_BLK)[:, None] >= jnp.arange(CUM_BLK)[None, :]).astype(bf16)
    lane_ids = jnp.arange(F_PAD, dtype=jnp.int32)
    sel = jnp.where(lane_ids < 3 * ATT_HEADS, lane_ids % 3, 3).reshape(1, F_PAD)
    q_scale = LOG2E / math.sqrt(HEAD_DIM)
    blk_ids = jnp.arange(ATT_BLK)
    causal = jnp.where(blk_ids[:, None] > blk_ids[None, :], NEG_BIG, 0.0).astype(f32)
    mask = jnp.stack([jnp.zeros_like(causal), causal])

    x2 = x.reshape(b * s, d)
    for l in range(depth):
        wl = w_in[l]
        wq, wk, wv = (wl[:, :o1].astype(bf16), wl[:, o1:o2].astype(bf16), wl[:, o2:o3].astype(bf16))
        wf = jnp.pad(jnp.repeat(wl[:, o3:o4], 3, axis=1), ((0, 0), (0, F_PAD - 3 * ATT_HEADS))).astype(bf16)
        wa, wg = wl[:, o4:o5].astype(bf16), wl[:, o5:].astype(bf16)
        bfp = jnp.pad(jnp.repeat(b_f[l], 3), (0, F_PAD - 3 * ATT_HEADS)).reshape(1, F_PAD)
        gq = (jnp.tile(q_norm_g[l], ATT_HEADS) * q_scale).reshape(1, ATT_WIDTH)
        gk = jnp.tile(k_norm_g[l], ATT_HEADS).reshape(1, ATT_WIDTH)

        qt3, k3, vt3, h3, lf3 = _in_proj(x2.reshape(b, s, d), norm1_g[l].reshape(1, d), wq, wk, wv, wa, wg, wf,
                                        bfp, gq, gk, gsum)
        cp3 = _cumsum(lf3, tri, sel)
        att = _attention(qt3, k3, cp3, vt3, mask)
        cw8 = jnp.broadcast_to(conv_w[l][:, None, :], (CONV_WIDTH, SUBLANES, CONV_CH))
        hc = _conv_module(h3, cw8, conv_b[l].reshape(1, CONV_CH),
                          conv_ln_g[l].reshape(1, CONV_CH), conv_ln_b[l].reshape(1, CONV_CH))
        wo = w_o[l].astype(bf16)
        x2 = _out_proj(x2, att.reshape(b * s, ATT_WIDTH), hc.reshape(b * s, CONV_CH),
                       wo[:ATT_WIDTH], wo[ATT_WIDTH:])
        x2 = _mlp(x2, norm2_g[l].reshape(1, d), w_mlp_in[l].astype(bf16), w_mlp_out[l].astype(bf16))
    return x2.reshape(b, s, d)
```

```python
import functools
import math

import jax
import jax.numpy as jnp
from jax import lax
from jax.experimental import pallas as pl
from jax.experimental.pallas import tpu as pltpu

D_MODEL = 1024
ATT_HEADS = 8
HEAD_DIM = 64
ATT_WIDTH = ATT_HEADS * HEAD_DIM
CONV_CH = D_MODEL - ATT_WIDTH
CONV_WIDTH = 31
D_FF = 4 * D_MODEL
EPS = 1e-6
LOG2E = 1.4426950408889634

LANES = 128
SUBLANES = 8
MXU_TILE = 256
F_PAD = LANES
HALO = 32
NEG_BIG = -1e30

TM_IN = 512
TM_MIX = 512
TS_CONV = 256
FF_CHUNK = 512
ATT_BLK = 512
ACC_ROWS = HEAD_DIM + 16
CONV_ROWS = 32
CUM_BLK = 256
VMEM_LIMIT = 56 * 1024 * 1024

bf16 = jnp.bfloat16
f32 = jnp.float32


def _dot(a, b):
    return jnp.dot(a, b, preferred_element_type=f32)


def _split2(y):
    hi = y.astype(bf16)
    lo = (y - hi.astype(f32)).astype(bf16)
    return hi, lo


def _split3(y):
    hi = y.astype(bf16)
    r = y - hi.astype(f32)
    mid = r.astype(bf16)
    lo = (r - mid.astype(f32)).astype(bf16)
    return hi, mid, lo


def _in_proj_kernel(x_ref, g1_ref, wq_ref, wk_ref, wv_ref, wa_ref, wg_ref, wf_ref, bf_ref,
                    gq_ref, gk_ref, gsum_ref, qt_ref, k_ref, vt_ref, h_ref, lf_ref):
    x = x_ref[0]
    ms = jnp.mean(x * x, axis=-1, keepdims=True)
    u = (x * lax.rsqrt(ms + EPS) * g1_ref[...]).astype(bf16)
    gsum = gsum_ref[...]

    def head_norm(y, gain):
        hi, lo = _split2(y * y)
        parts = []
        for c in range(ATT_WIDTH // MXU_TILE):
            sl = slice(c * MXU_TILE, (c + 1) * MXU_TILE)
            parts.append(_dot(hi[:, sl], gsum) + _dot(lo[:, sl], gsum))
        ss = jnp.concatenate(parts, axis=-1)
        return y * lax.rsqrt(ss * (1.0 / HEAD_DIM) + EPS) * gain

    qt_ref[0] = head_norm(_dot(u, wq_ref[...]), gq_ref[...]).T.astype(bf16)
    k_ref[0] = head_norm(_dot(u, wk_ref[...]), gk_ref[...]).astype(bf16)
    vt_ref[0] = _dot(u, wv_ref[...]).T.astype(bf16)
    a = _dot(u, wa_ref[...])
    g = _dot(u, wg_ref[...])
    h_ref[0] = a * jax.nn.sigmoid(g)
    lf_ref[0] = jax.nn.log_sigmoid(_dot(u, wf_ref[...]) + bf_ref[...])


def _in_proj(x3, g1, wq, wk, wv, wa, wg, wf, bfp, gq, gk, gsum):
    b, s, _ = x3.shape
    row = lambda n: pl.BlockSpec((1, TM_IN, n), lambda bi, i: (bi, i, 0))
    col = pl.BlockSpec((1, ATT_WIDTH, TM_IN), lambda bi, i: (bi, 0, i))
    full = lambda a: pl.BlockSpec(a.shape, lambda bi, i: (0,) * a.ndim)
    return pl.pallas_call(
        _in_proj_kernel,
        grid=(b, s // TM_IN),
        in_specs=[row(D_MODEL)] + [full(a) for a in (g1, wq, wk, wv, wa, wg, wf, bfp, gq, gk, gsum)],
        out_specs=[col, row(ATT_WIDTH), col, row(CONV_CH), row(F_PAD)],
        out_shape=[jax.ShapeDtypeStruct((b, ATT_WIDTH, s), bf16), jax.ShapeDtypeStruct((b, s, ATT_WIDTH), bf16),
                   jax.ShapeDtypeStruct((b, ATT_WIDTH, s), bf16),
                   jax.ShapeDtypeStruct((b, s, CONV_CH), f32), jax.ShapeDtypeStruct((b, s, F_PAD), f32)],
        compiler_params=pltpu.CompilerParams(dimension_semantics=("parallel", "parallel"),
                                             vmem_limit_bytes=VMEM_LIMIT),
        name="in_proj",
    )(x3, g1, wq, wk, wv, wa, wg, wf, bfp, gq, gk, gsum)


def _cumsum_kernel(lf_ref, tri_ref, sel_ref, cp_ref):
    s = lf_ref.shape[1]
    tri = tri_ref[...]
    sel = jnp.broadcast_to(sel_ref[...], (CUM_BLK, F_PAD))
    carry = jnp.zeros((1, F_PAD), f32)
    for j in range(s // CUM_BLK):
        rows = slice(j * CUM_BLK, (j + 1) * CUM_BLK)
        hi, mid, lo = _split3(lf_ref[0, rows, :])
        cs = _dot(tri, hi) + _dot(tri, mid) + _dot(tri, lo) + carry
        carry = cs[CUM_BLK - 1:CUM_BLK, :]
        bh, bm, bl = _split3(cs * (-LOG2E))
        piece = jnp.where(sel == 0, bh.astype(f32), jnp.where(sel == 1, bm.astype(f32), bl.astype(f32)))
        cp_ref[0, rows, :] = jnp.where(sel < 3, piece, 0.0).astype(bf16)


def _cumsum(lf3, tri, sel):
    b, s, _ = lf3.shape
    return pl.pallas_call(
        _cumsum_kernel,
        grid=(b,),
        in_specs=[pl.BlockSpec((1, s, F_PAD), lambda i: (i, 0, 0)),
                  pl.BlockSpec(tri.shape, lambda i: (0, 0)),
                  pl.BlockSpec(sel.shape, lambda i: (0, 0))],
        out_specs=pl.BlockSpec((1, s, F_PAD), lambda i: (i, 0, 0)),
        out_shape=jax.ShapeDtypeStruct((b, s, F_PAD), bf16),
        compiler_params=pltpu.CompilerParams(dimension_semantics=("parallel",)),
        name="forget_cumsum",
    )(lf3, tri, sel)


def _attn_kernel(qt_ref, k_ref, cp_ref, vt_ref, mask_ref, o_ref, acc_ref, st_a, st_b, p_a, p_b):
    hp = pl.program_id(1)
    nblk = k_ref.shape[1] // ATT_BLK
    rows = [slice(HEAD_DIM * h, HEAD_DIM * (h + 1)) for h in range(2)]
    row_id = lax.broadcasted_iota(jnp.int32, (LANES, ATT_BLK), 0)
    zeros_t = jnp.zeros((HEAD_DIM, ATT_BLK), bf16)
    ones_t = [jnp.where((row_id >= 3 * (2 * hp + h)) & (row_id < 3 * (2 * hp + h) + 3), 1.0, 0.0).astype(bf16)
              for h in range(2)]

    def blk(i):
        return pl.ds(pl.multiple_of(i * ATT_BLK, ATT_BLK), ATT_BLK)

    def scores(qi, kj, st_ref):
        kc = jnp.concatenate([k_ref[0, blk(kj), :], cp_ref[0, blk(kj), :]], axis=1)
        qt = qt_ref[0, :, blk(qi)]
        mask = mask_ref[(qi == kj).astype(jnp.int32)]
        mx = []
        for h in range(2):
            parts = [qt[rows[0], :], zeros_t] if h == 0 else [zeros_t, qt[rows[1], :]]
            st = _dot(kc, jnp.concatenate(parts + [ones_t[h]], axis=0)) + mask
            st_ref[h] = st
            mx.append(jnp.max(st, axis=0, keepdims=True))
        return mx

    def advance(qi, kj):
        last = kj == qi
        qn = jnp.minimum(jnp.where(last, qi + 1, qi), nblk - 1)
        kn = jnp.minimum(jnp.where(last, 0, kj + 1), qn)
        return qn, kn

    ones_l = jnp.ones((ACC_ROWS - HEAD_DIM, ATT_BLK), bf16)

    def weighted_values(kj, p_ref):
        return [_dot(jnp.concatenate([vt_ref[0, rows[h], blk(kj)], ones_l], axis=0), p_ref[h]) for h in range(2)]

    def finish_prev(h, a_prev, pv_prev):
        acc = a_prev * acc_ref[h] + pv_prev
        acc_ref[h] = acc
        return acc[0:HEAD_DIM, :] * (1.0 / acc[HEAD_DIM:HEAD_DIM + 1, :])

    def half_step(carry, st_cur, st_next, p_cur, p_prev):
        qi, kj, qi_prev, kj_prev = carry[:4]
        pv_prev = weighted_values(kj_prev, p_prev)
        qn, kn = advance(qi, kj)
        mx_next = scores(qn, kn, st_next)
        first = kj == 0
        out, o_t = [], []
        for h in range(2):
            m_old, a_prev, mx = carry[4 + 3 * h:7 + 3 * h]
            o_t.append(finish_prev(h, a_prev, pv_prev[h]))
            m_old = jnp.where(first, NEG_BIG, m_old)
            m_new = jnp.maximum(m_old, mx)
            p_cur[h] = jnp.exp2(st_cur[h] - m_new).astype(bf16)
            out += [m_new, jnp.exp2(m_old - m_new), mx_next[h]]
        o_ref[0, blk(qi_prev), :] = jnp.concatenate(o_t, axis=0).T.astype(o_ref.dtype)
        return (qn, kn, qi, kj) + tuple(out)

    acc_ref[...] = jnp.ones(acc_ref.shape, f32)
    p_b[...] = jnp.zeros(p_b.shape, bf16)
    zero = jnp.int32(0)
    mx0 = scores(zero, zero, st_a)
    init = (zero, zero, zero, zero)
    for h in range(2):
        init += (jnp.full((1, ATT_BLK), NEG_BIG, f32), jnp.ones((1, ATT_BLK), f32), mx0[h])

    def body(_, carry):
        carry = half_step(carry, st_a, st_b, p_a, p_b)
        return half_step(carry, st_b, st_a, p_b, p_a)

    npairs = nblk * (nblk + 1) // 2
    carry = lax.fori_loop(0, npairs // 2, body, init)

    pv = weighted_values(nblk - 1, p_b)
    o_t = [finish_prev(h, carry[5 + 3 * h], pv[h]) for h in range(2)]
    o_ref[0, blk(nblk - 1), :] = jnp.concatenate(o_t, axis=0).T.astype(o_ref.dtype)


def _attention(qt3, k3, cp3, vt3, mask):
    b, s, _ = k3.shape
    nblk = s // ATT_BLK
    assert (nblk * (nblk + 1) // 2) % 2 == 0, "pair loop is unrolled by two"
    hp = ATT_WIDTH // LANES
    return pl.pallas_call(
        _attn_kernel,
        grid=(b, hp),
        in_specs=[pl.BlockSpec((1, LANES, s), lambda bi, hi: (bi, hi, 0)),
                  pl.BlockSpec((1, s, LANES), lambda bi, hi: (bi, 0, hi)),
                  pl.BlockSpec((1, s, F_PAD), lambda bi, hi: (bi, 0, 0)),
                  pl.BlockSpec((1, LANES, s), lambda bi, hi: (bi, hi, 0)),
                  pl.BlockSpec(mask.shape, lambda bi, hi: (0, 0, 0))],
        out_specs=pl.BlockSpec((1, s, LANES), lambda bi, hi: (bi, 0, hi)),
        out_shape=jax.ShapeDtypeStruct((b, s, ATT_WIDTH), bf16),
        scratch_shapes=[pltpu.VMEM((2, ACC_ROWS, ATT_BLK), f32),
                        pltpu.VMEM((2, ATT_BLK, ATT_BLK), f32), pltpu.VMEM((2, ATT_BLK, ATT_BLK), f32),
                        pltpu.VMEM((2, ATT_BLK, ATT_BLK), bf16), pltpu.VMEM((2, ATT_BLK, ATT_BLK), bf16)],
        compiler_params=pltpu.CompilerParams(dimension_semantics=("parallel", "parallel"),
                                             vmem_limit_bytes=VMEM_LIMIT),
        name="fox_attention",
    )(qt3, k3, cp3, vt3, mask)


def _conv_module(h_ref, halo, cw_ref, cb_ref, lg_ref, lb_ref, win_ref, sh_ref, hc_ref):
    win_ref[0:HALO, :] = halo
    win_ref[HALO:HALO + TS_CONV, :] = h_ref[...]
    n_sh = sh_ref.shape[1]
    for r in range(1, SUBLANES):
        sh_ref[r] = win_ref[r:r + n_sh, :]
    off = HALO - (CONV_WIDTH - 1)
    for c in range(TS_CONV // CONV_ROWS):
        acc = None
        for w in range(CONV_WIDTH):
            r = (off + w) % SUBLANES
            base = c * CONV_ROWS + (off + w) - r
            src = win_ref if r == 0 else sh_ref.at[r]
            x = src[base:base + CONV_ROWS, :].reshape(CONV_ROWS // SUBLANES, SUBLANES, -1)
            term = (x * cw_ref[w]).reshape(CONV_ROWS, -1)
            acc = term if acc is None else acc + term
        y = acc + cb_ref[...]
        mu = jnp.mean(y, axis=-1, keepdims=True)
        d = y - mu
        var = jnp.mean(d * d, axis=-1, keepdims=True)
        z = d * lax.rsqrt(var + EPS) * lg_ref[...] + lb_ref[...]
        hc_ref[c * CONV_ROWS:(c + 1) * CONV_ROWS, :] = (z * jax.nn.sigmoid(z)).astype(hc_ref.dtype)


def _conv_kernel(h_ref, hh_ref, cw_ref, cb_ref, lg_ref, lb_ref, o_ref, win_ref, sh_ref):
    i = pl.program_id(1)
    halo = hh_ref[0]
    halo = jnp.where(i > 0, halo, jnp.zeros_like(halo))
    _conv_module(h_ref.at[0], halo, cw_ref, cb_ref, lg_ref, lb_ref, win_ref, sh_ref, o_ref.at[0])


def _conv_call(h3, cw8, cb, lg, lb):
    b, s, c = h3.shape
    ratio = TS_CONV // HALO
    const = lambda a: pl.BlockSpec(a.shape, lambda bi, i: (0,) * a.ndim)
    return pl.pallas_call(
        _conv_kernel,
        grid=(b, s // TS_CONV),
        in_specs=[pl.BlockSpec((1, TS_CONV, c), lambda bi, i: (bi, i, 0)),
                  pl.BlockSpec((1, HALO, c), lambda bi, i: (bi, jnp.maximum(i * ratio - 1, 0), 0)),
                  const(cw8), const(cb), const(lg), const(lb)],
        out_specs=pl.BlockSpec((1, TS_CONV, c), lambda bi, i: (bi, i, 0)),
        out_shape=jax.ShapeDtypeStruct((b, s, c), bf16),
        scratch_shapes=[pltpu.VMEM((HALO + TS_CONV, c), f32),
                        pltpu.VMEM((SUBLANES, HALO + TS_CONV - SUBLANES, c), f32)],
        compiler_params=pltpu.CompilerParams(dimension_semantics=("parallel", "parallel")),
        name="conv_module",
    )(h3, h3, cw8, cb, lg, lb)


def _mix_mlp_kernel(x_ref, att_ref, hc_ref, woa_ref, woc_ref, g2_ref, w1_ref, w2_ref, o_ref, acc_ref):
    o_ref[...] = x_ref[...] + _dot(att_ref[...], woa_ref[...]) + _dot(hc_ref[...], woc_ref[...])
    x1 = o_ref[...]
    ms = jnp.mean(x1 * x1, axis=-1, keepdims=True)
    u = (x1 * lax.rsqrt(ms + EPS) * g2_ref[...]).astype(bf16)
    for c in range(D_FF // FF_CHUNK):
        sl = slice(c * FF_CHUNK, (c + 1) * FF_CHUNK)
        h = jnp.maximum(_dot(u, w1_ref[:, sl]), 0.0)
        part = _dot((h * h).astype(bf16), w2_ref[sl, :])
        if c == 0:
            acc_ref[...] = part
        else:
            acc_ref[...] += part
    o_ref[...] += acc_ref[...]


def _mix_mlp(x2, att2, hc2, woa, woc, g2, w1, w2):
    t = x2.shape[0]
    row = lambda w: pl.BlockSpec((TM_MIX, w), lambda i: (i, 0))
    const = lambda a: pl.BlockSpec(a.shape, lambda i: (0,) * a.ndim, pipeline_mode=pl.Buffered(1))
    return pl.pallas_call(
        _mix_mlp_kernel,
        grid=(t // TM_MIX,),
        in_specs=[row(D_MODEL), row(ATT_WIDTH), row(CONV_CH), const(woa), const(woc), const(g2), const(w1), const(w2)],
        out_specs=row(D_MODEL),
        out_shape=jax.ShapeDtypeStruct((t, D_MODEL), f32),
        scratch_shapes=[pltpu.VMEM((TM_MIX, D_MODEL), f32)],
        compiler_params=pltpu.CompilerParams(dimension_semantics=("parallel",), vmem_limit_bytes=VMEM_LIMIT),
        name="mix_mlp",
    )(x2, att2, hc2, woa, woc, g2, w1, w2)


def kernel(x, norm1_g, w_in, b_f, q_norm_g, k_norm_g, conv_w, conv_b, conv_ln_g, conv_ln_b, w_o,
           norm2_g, w_mlp_in, w_mlp_out):
    b, s, d = x.shape
    depth = w_in.shape[0]
    o1, o2, o3 = ATT_WIDTH, 2 * ATT_WIDTH, 3 * ATT_WIDTH
    o4 = o3 + ATT_HEADS
    o5 = o4 + CONV_CH

    idx = jnp.arange(MXU_TILE)
    gsum = (idx[:, None] // HEAD_DIM == idx[None, :] // HEAD_DIM).astype(bf16)
    tri = (jnp.arange(CUM_BLK)[:, None] >= jnp.arange(CUM_BLK)[None, :]).astype(bf16)
    lane_ids = jnp.arange(F_PAD, dtype=jnp.int32)
    sel = jnp.where(lane_ids < 3 * ATT_HEADS, lane_ids % 3, 3).reshape(1, F_PAD)
    q_scale = LOG2E / math.sqrt(HEAD_DIM)
    blk_ids = jnp.arange(ATT_BLK)
    causal = jnp.where(blk_ids[:, None] > blk_ids[None, :], NEG_BIG, 0.0).astype(f32)
    mask = jnp.stack([jnp.zeros_like(causal), causal])

    x2 = x.reshape(b * s, d)
    for l in range(depth):
        wl = w_in[l]
        wq, wk, wv = (wl[:, :o1].astype(bf16), wl[:, o1:o2].astype(bf16), wl[:, o2:o3].astype(bf16))
        wf = jnp.pad(jnp.repeat(wl[:, o3:o4], 3, axis=1), ((0, 0), (0, F_PAD - 3 * ATT_HEADS))).astype(bf16)
        wa, wg = wl[:, o4:o5].astype(bf16), wl[:, o5:].astype(bf16)
        bfp = jnp.pad(jnp.repeat(b_f[l], 3), (0, F_PAD - 3 * ATT_HEADS)).reshape(1, F_PAD)
        gq = (jnp.tile(q_norm_g[l], ATT_HEADS) * q_scale).reshape(1, ATT_WIDTH)
        gk = jnp.tile(k_norm_g[l], ATT_HEADS).reshape(1, ATT_WIDTH)

        qt3, k3, vt3, h3, lf3 = _in_proj(x2.reshape(b, s, d), norm1_g[l].reshape(1, d), wq, wk, wv, wa, wg, wf,
                                        bfp, gq, gk, gsum)
        cp3 = _cumsum(lf3, tri, sel)
        att = _attention(qt3, k3, cp3, vt3, mask)
        cw8 = jnp.broadcast_to(conv_w[l][:, None, :], (CONV_WIDTH, SUBLANES, CONV_CH))
        wo = w_o[l].astype(bf16)
        hc = _conv_call(h3, cw8, conv_b[l].reshape(1, CONV_CH),
                        conv_ln_g[l].reshape(1, CONV_CH), conv_ln_b[l].reshape(1, CONV_CH))
        x2 = _mix_mlp(x2, att.reshape(b * s, ATT_WIDTH), hc.reshape(b * s, CONV_CH), wo[:ATT_WIDTH], wo[ATT_WIDTH:],
                      norm2_g[l].reshape(1, d), w_mlp_in[l].astype(bf16), w_mlp_out[l].astype(bf16))
    return x2.reshape(b, s, d)
```

```python
import functools
import math

import jax
import jax.numpy as jnp
from jax import lax
from jax.experimental import pallas as pl
from jax.experimental.pallas import tpu as pltpu

D_MODEL = 1024
ATT_HEADS = 8
HEAD_DIM = 64
ATT_WIDTH = ATT_HEADS * HEAD_DIM
CONV_CH = D_MODEL - ATT_WIDTH
CONV_WIDTH = 31
D_FF = 4 * D_MODEL
EPS = 1e-6
LOG2E = 1.4426950408889634

LANES = 128
SUBLANES = 8
MXU_TILE = 256
F_PAD = LANES
HALO = 32
NEG_BIG = -1e30

TM_IN = 512
TM_MIX = 512
FF_CHUNK = 512
ATT_BLK = 512
ACC_ROWS = HEAD_DIM + 16
CONV_ROWS = 32
CUM_BLK = 256
VMEM_LIMIT = 56 * 1024 * 1024

bf16 = jnp.bfloat16
f32 = jnp.float32


def _dot(a, b):
    return jnp.dot(a, b, preferred_element_type=f32)


def _split2(y):
    hi = y.astype(bf16)
    lo = (y - hi.astype(f32)).astype(bf16)
    return hi, lo


def _split3(y):
    hi = y.astype(bf16)
    r = y - hi.astype(f32)
    mid = r.astype(bf16)
    lo = (r - mid.astype(f32)).astype(bf16)
    return hi, mid, lo


def _in_proj_kernel(x_ref, g1_ref, wq_ref, wk_ref, wv_ref, wa_ref, wg_ref, wf_ref, bf_ref,
                    gq_ref, gk_ref, gsum_ref, qt_ref, k_ref, vt_ref, h_ref, lf_ref):
    x = x_ref[0]
    ms = jnp.mean(x * x, axis=-1, keepdims=True)
    u = (x * lax.rsqrt(ms + EPS) * g1_ref[...]).astype(bf16)
    gsum = gsum_ref[...]

    def head_norm(y, gain):
        hi, lo = _split2(y * y)
        parts = []
        for c in range(ATT_WIDTH // MXU_TILE):
            sl = slice(c * MXU_TILE, (c + 1) * MXU_TILE)
            parts.append(_dot(hi[:, sl], gsum) + _dot(lo[:, sl], gsum))
        ss = jnp.concatenate(parts, axis=-1)
        return y * lax.rsqrt(ss * (1.0 / HEAD_DIM) + EPS) * gain

    qt_ref[0] = head_norm(_dot(u, wq_ref[...]), gq_ref[...]).T.astype(bf16)
    k_ref[0] = head_norm(_dot(u, wk_ref[...]), gk_ref[...]).astype(bf16)
    vt_ref[0] = _dot(u, wv_ref[...]).T.astype(bf16)
    a = _dot(u, wa_ref[...])
    g = _dot(u, wg_ref[...])
    h_ref[0] = a * jax.nn.sigmoid(g)
    lf_ref[0] = jax.nn.log_sigmoid(_dot(u, wf_ref[...]) + bf_ref[...])


def _in_proj(x3, g1, wq, wk, wv, wa, wg, wf, bfp, gq, gk, gsum):
    b, s, _ = x3.shape
    row = lambda n: pl.BlockSpec((1, TM_IN, n), lambda bi, i: (bi, i, 0))
    col = pl.BlockSpec((1, ATT_WIDTH, TM_IN), lambda bi, i: (bi, 0, i))
    full = lambda a: pl.BlockSpec(a.shape, lambda bi, i: (0,) * a.ndim)
    return pl.pallas_call(
        _in_proj_kernel,
        grid=(b, s // TM_IN),
        in_specs=[row(D_MODEL)] + [full(a) for a in (g1, wq, wk, wv, wa, wg, wf, bfp, gq, gk, gsum)],
        out_specs=[col, row(ATT_WIDTH), col, row(CONV_CH), row(F_PAD)],
        out_shape=[jax.ShapeDtypeStruct((b, ATT_WIDTH, s), bf16), jax.ShapeDtypeStruct((b, s, ATT_WIDTH), bf16),
                   jax.ShapeDtypeStruct((b, ATT_WIDTH, s), bf16),
                   jax.ShapeDtypeStruct((b, s, CONV_CH), f32), jax.ShapeDtypeStruct((b, s, F_PAD), f32)],
        compiler_params=pltpu.CompilerParams(dimension_semantics=("parallel", "parallel"),
                                             vmem_limit_bytes=VMEM_LIMIT),
        name="in_proj",
    )(x3, g1, wq, wk, wv, wa, wg, wf, bfp, gq, gk, gsum)


def _cumsum_kernel(lf_ref, tri_ref, sel_ref, cp_ref):
    s = lf_ref.shape[1]
    tri = tri_ref[...]
    sel = jnp.broadcast_to(sel_ref[...], (CUM_BLK, F_PAD))
    carry = jnp.zeros((1, F_PAD), f32)
    for j in range(s // CUM_BLK):
        rows = slice(j * CUM_BLK, (j + 1) * CUM_BLK)
        hi, mid, lo = _split3(lf_ref[0, rows, :])
        cs = _dot(tri, hi) + _dot(tri, mid) + _dot(tri, lo) + carry
        carry = cs[CUM_BLK - 1:CUM_BLK, :]
        bh, bm, bl = _split3(cs * (-LOG2E))
        piece = jnp.where(sel == 0, bh.astype(f32), jnp.where(sel == 1, bm.astype(f32), bl.astype(f32)))
        cp_ref[0, rows, :] = jnp.where(sel < 3, piece, 0.0).astype(bf16)


def _cumsum(lf3, tri, sel):
    b, s, _ = lf3.shape
    return pl.pallas_call(
        _cumsum_kernel,
        grid=(b,),
        in_specs=[pl.BlockSpec((1, s, F_PAD), lambda i: (i, 0, 0)),
                  pl.BlockSpec(tri.shape, lambda i: (0, 0)),
                  pl.BlockSpec(sel.shape, lambda i: (0, 0))],
        out_specs=pl.BlockSpec((1, s, F_PAD), lambda i: (i, 0, 0)),
        out_shape=jax.ShapeDtypeStruct((b, s, F_PAD), bf16),
        compiler_params=pltpu.CompilerParams(dimension_semantics=("parallel",)),
        name="forget_cumsum",
    )(lf3, tri, sel)


def _attn_kernel(qt_ref, k_ref, cp_ref, vt_ref, mask_ref, o_ref, acc_ref, st_a, st_b, p_a, p_b):
    hp = pl.program_id(1)
    nblk = k_ref.shape[1] // ATT_BLK
    rows = [slice(HEAD_DIM * h, HEAD_DIM * (h + 1)) for h in range(2)]
    row_id = lax.broadcasted_iota(jnp.int32, (LANES, ATT_BLK), 0)
    zeros_t = jnp.zeros((HEAD_DIM, ATT_BLK), bf16)
    ones_t = [jnp.where((row_id >= 3 * (2 * hp + h)) & (row_id < 3 * (2 * hp + h) + 3), 1.0, 0.0).astype(bf16)
              for h in range(2)]

    def blk(i):
        return pl.ds(pl.multiple_of(i * ATT_BLK, ATT_BLK), ATT_BLK)

    def scores(qi, kj, st_ref):
        kc = jnp.concatenate([k_ref[0, blk(kj), :], cp_ref[0, blk(kj), :]], axis=1)
        qt = qt_ref[0, :, blk(qi)]
        mask = mask_ref[(qi == kj).astype(jnp.int32)]
        mx = []
        for h in range(2):
            parts = [qt[rows[0], :], zeros_t] if h == 0 else [zeros_t, qt[rows[1], :]]
            st = _dot(kc, jnp.concatenate(parts + [ones_t[h]], axis=0)) + mask
            st_ref[h] = st
            mx.append(jnp.max(st, axis=0, keepdims=True))
        return mx

    def advance(qi, kj):
        last = kj == qi
        qn = jnp.minimum(jnp.where(last, qi + 1, qi), nblk - 1)
        kn = jnp.minimum(jnp.where(last, 0, kj + 1), qn)
        return qn, kn

    ones_l = jnp.ones((ACC_ROWS - HEAD_DIM, ATT_BLK), bf16)

    def weighted_values(kj, p_ref):
        return [_dot(jnp.concatenate([vt_ref[0, rows[h], blk(kj)], ones_l], axis=0), p_ref[h]) for h in range(2)]

    def finish_prev(h, a_prev, pv_prev):
        acc = a_prev * acc_ref[h] + pv_prev
        acc_ref[h] = acc
        return acc[0:HEAD_DIM, :] * (1.0 / acc[HEAD_DIM:HEAD_DIM + 1, :])

    def half_step(carry, st_cur, st_next, p_cur, p_prev):
        qi, kj, qi_prev, kj_prev = carry[:4]
        pv_prev = weighted_values(kj_prev, p_prev)
        qn, kn = advance(qi, kj)
        mx_next = scores(qn, kn, st_next)
        first = kj == 0
        out, o_t = [], []
        for h in range(2):
            m_old, a_prev, mx = carry[4 + 3 * h:7 + 3 * h]
            o_t.append(finish_prev(h, a_prev, pv_prev[h]))
            m_old = jnp.where(first, NEG_BIG, m_old)
            m_new = jnp.maximum(m_old, mx)
            p_cur[h] = jnp.exp2(st_cur[h] - m_new).astype(bf16)
            out += [m_new, jnp.exp2(m_old - m_new), mx_next[h]]
        o_ref[0, blk(qi_prev), :] = jnp.concatenate(o_t, axis=0).T.astype(o_ref.dtype)
        return (qn, kn, qi, kj) + tuple(out)

    acc_ref[...] = jnp.ones(acc_ref.shape, f32)
    p_b[...] = jnp.zeros(p_b.shape, bf16)
    zero = jnp.int32(0)
    mx0 = scores(zero, zero, st_a)
    init = (zero, zero, zero, zero)
    for h in range(2):
        init += (jnp.full((1, ATT_BLK), NEG_BIG, f32), jnp.ones((1, ATT_BLK), f32), mx0[h])

    def body(_, carry):
        carry = half_step(carry, st_a, st_b, p_a, p_b)
        return half_step(carry, st_b, st_a, p_b, p_a)

    npairs = nblk * (nblk + 1) // 2
    carry = lax.fori_loop(0, npairs // 2, body, init)

    pv = weighted_values(nblk - 1, p_b)
    o_t = [finish_prev(h, carry[5 + 3 * h], pv[h]) for h in range(2)]
    o_ref[0, blk(nblk - 1), :] = jnp.concatenate(o_t, axis=0).T.astype(o_ref.dtype)


def _attention(qt3, k3, cp3, vt3, mask):
    b, s, _ = k3.shape
    nblk = s // ATT_BLK
    assert (nblk * (nblk + 1) // 2) % 2 == 0, "pair loop is unrolled by two"
    hp = ATT_WIDTH // LANES
    return pl.pallas_call(
        _attn_kernel,
        grid=(b, hp),
        in_specs=[pl.BlockSpec((1, LANES, s), lambda bi, hi: (bi, hi, 0)),
                  pl.BlockSpec((1, s, LANES), lambda bi, hi: (bi, 0, hi)),
                  pl.BlockSpec((1, s, F_PAD), lambda bi, hi: (bi, 0, 0)),
                  pl.BlockSpec((1, LANES, s), lambda bi, hi: (bi, hi, 0)),
                  pl.BlockSpec(mask.shape, lambda bi, hi: (0, 0, 0))],
        out_specs=pl.BlockSpec((1, s, LANES), lambda bi, hi: (bi, 0, hi)),
        out_shape=jax.ShapeDtypeStruct((b, s, ATT_WIDTH), bf16),
        scratch_shapes=[pltpu.VMEM((2, ACC_ROWS, ATT_BLK), f32),
                        pltpu.VMEM((2, ATT_BLK, ATT_BLK), f32), pltpu.VMEM((2, ATT_BLK, ATT_BLK), f32),
                        pltpu.VMEM((2, ATT_BLK, ATT_BLK), bf16), pltpu.VMEM((2, ATT_BLK, ATT_BLK), bf16)],
        compiler_params=pltpu.CompilerParams(dimension_semantics=("parallel", "parallel"),
                                             vmem_limit_bytes=VMEM_LIMIT),
        name="fox_attention",
    )(qt3, k3, cp3, vt3, mask)


def _conv_module(h_ref, halo, cw_ref, cb_ref, lg_ref, lb_ref, win_ref, sh_ref, hc_ref):
    n_rows = h_ref.shape[0]
    win_ref[0:HALO, :] = halo
    win_ref[HALO:HALO + n_rows, :] = h_ref[...]
    n_sh = sh_ref.shape[1]
    for r in range(1, SUBLANES):
        sh_ref[r] = win_ref[r:r + n_sh, :]
    off = HALO - (CONV_WIDTH - 1)
    for c in range(n_rows // CONV_ROWS):
        acc = None
        for w in range(CONV_WIDTH):
            r = (off + w) % SUBLANES
            base = c * CONV_ROWS + (off + w) - r
            src = win_ref if r == 0 else sh_ref.at[r]
            x = src[base:base + CONV_ROWS, :].reshape(CONV_ROWS // SUBLANES, SUBLANES, -1)
            term = (x * cw_ref[w]).reshape(CONV_ROWS, -1)
            acc = term if acc is None else acc + term
        y = acc + cb_ref[...]
        mu = jnp.mean(y, axis=-1, keepdims=True)
        d = y - mu
        var = jnp.mean(d * d, axis=-1, keepdims=True)
        z = d * lax.rsqrt(var + EPS) * lg_ref[...] + lb_ref[...]
        hc_ref[c * CONV_ROWS:(c + 1) * CONV_ROWS, :] = (z * jax.nn.sigmoid(z)).astype(hc_ref.dtype)


def _mix_mlp_kernel(x_ref, att_ref, h_ref, hh_ref, cw_ref, cb_ref, lg_ref, lb_ref, woa_ref, woc_ref,
                    g2_ref, w1_ref, w2_ref, o_ref, hc_ref, win_ref, sh_ref, acc_ref, *, tiles_per_seq):
    i = pl.program_id(0)
    halo = hh_ref[...]
    halo = jnp.where(i % tiles_per_seq == 0, jnp.zeros_like(halo), halo)
    _conv_module(h_ref, halo, cw_ref, cb_ref, lg_ref, lb_ref, win_ref, sh_ref, hc_ref)

    o_ref[...] = x_ref[...] + _dot(att_ref[...], woa_ref[...]) + _dot(hc_ref[...], woc_ref[...])
    x1 = o_ref[...]
    ms = jnp.mean(x1 * x1, axis=-1, keepdims=True)
    u = (x1 * lax.rsqrt(ms + EPS) * g2_ref[...]).astype(bf16)
    for c in range(D_FF // FF_CHUNK):
        sl = slice(c * FF_CHUNK, (c + 1) * FF_CHUNK)
        h = jnp.maximum(_dot(u, w1_ref[:, sl]), 0.0)
        part = _dot((h * h).astype(bf16), w2_ref[sl, :])
        if c == 0:
            acc_ref[...] = part
        else:
            acc_ref[...] += part
    o_ref[...] += acc_ref[...]


def _mix_mlp(x2, att2, h2, cw8, cb, lg, lb, woa, woc, g2, w1, w2, seq_len):
    t = x2.shape[0]
    ratio = TM_MIX // HALO
    row = lambda w: pl.BlockSpec((TM_MIX, w), lambda i: (i, 0))
    const = lambda a: pl.BlockSpec(a.shape, lambda i: (0,) * a.ndim, pipeline_mode=pl.Buffered(1))
    return pl.pallas_call(
        functools.partial(_mix_mlp_kernel, tiles_per_seq=seq_len // TM_MIX),
        grid=(t // TM_MIX,),
        in_specs=[row(D_MODEL), row(ATT_WIDTH), row(CONV_CH),
                  pl.BlockSpec((HALO, CONV_CH), lambda i: (jnp.maximum(i * ratio - 1, 0), 0)),
                  const(cw8), const(cb), const(lg), const(lb), const(woa), const(woc),
                  const(g2), const(w1), const(w2)],
        out_specs=row(D_MODEL),
        out_shape=jax.ShapeDtypeStruct((t, D_MODEL), f32),
        scratch_shapes=[pltpu.VMEM((TM_MIX, CONV_CH), bf16),
                        pltpu.VMEM((HALO + TM_MIX, CONV_CH), f32),
                        pltpu.VMEM((SUBLANES, HALO + TM_MIX - SUBLANES, CONV_CH), f32),
                        pltpu.VMEM((TM_MIX, D_MODEL), f32)],
        compiler_params=pltpu.CompilerParams(dimension_semantics=("parallel",), vmem_limit_bytes=VMEM_LIMIT),
        name="mix_mlp",
    )(x2, att2, h2, h2, cw8, cb, lg, lb, woa, woc, g2, w1, w2)


def kernel(x, norm1_g, w_in, b_f, q_norm_g, k_norm_g, conv_w, conv_b, conv_ln_g, conv_ln_b, w_o,
           norm2_g, w_mlp_in, w_mlp_out):
    b, s, d = x.shape
    depth = w_in.shape[0]
    o1, o2, o3 = ATT_WIDTH, 2 * ATT_WIDTH, 3 * ATT_WIDTH
    o4 = o3 + ATT_HEADS
    o5 = o4 + CONV_CH

    idx = jnp.arange(MXU_TILE)
    gsum = (idx[:, None] // HEAD_DIM == idx[None, :] // HEAD_DIM).astype(bf16)
    tri = (jnp.arange(CUM_BLK)[:, None] >= jnp.arange(CUM_BLK)[None, :]).astype(bf16)
    lane_ids = jnp.arange(F_PAD, dtype=jnp.int32)
    sel = jnp.where(lane_ids < 3 * ATT_HEADS, lane_ids % 3, 3).reshape(1, F_PAD)
    q_scale = LOG2E / math.sqrt(HEAD_DIM)
    blk_ids = jnp.arange(ATT_BLK)
    causal = jnp.where(blk_ids[:, None] > blk_ids[None, :], NEG_BIG, 0.0).astype(f32)
    mask = jnp.stack([jnp.zeros_like(causal), causal])

    x2 = x.reshape(b * s, d)
    for l in range(depth):
        wl = w_in[l]
        wq, wk, wv = (wl[:, :o1].astype(bf16), wl[:, o1:o2].astype(bf16), wl[:, o2:o3].astype(bf16))
        wf = jnp.pad(jnp.repeat(wl[:, o3:o4], 3, axis=1), ((0, 0), (0, F_PAD - 3 * ATT_HEADS))).astype(bf16)
        wa, wg = wl[:, o4:o5].astype(bf16), wl[:, o5:].astype(bf16)
        bfp = jnp.pad(jnp.repeat(b_f[l], 3), (0, F_PAD - 3 * ATT_HEADS)).reshape(1, F_PAD)
        gq = (jnp.tile(q_norm_g[l], ATT_HEADS) * q_scale).reshape(1, ATT_WIDTH)
        gk = jnp.tile(k_norm_g[l], ATT_HEADS).reshape(1, ATT_WIDTH)

        qt3, k3, vt3, h3, lf3 = _in_proj(x2.reshape(b, s, d), norm1_g[l].reshape(1, d), wq, wk, wv, wa, wg, wf,
                                        bfp, gq, gk, gsum)
        cp3 = _cumsum(lf3, tri, sel)
        att = _attention(qt3, k3, cp3, vt3, mask)
        cw8 = jnp.broadcast_to(conv_w[l][:, None, :], (CONV_WIDTH, SUBLANES, CONV_CH))
        wo = w_o[l].astype(bf16)
        x2 = _mix_mlp(x2, att.reshape(b * s, ATT_WIDTH), h3.reshape(b * s, CONV_CH), cw8,
                      conv_b[l].reshape(1, CONV_CH), conv_ln_g[l].reshape(1, CONV_CH),
                      conv_ln_b[l].reshape(1, CONV_CH), wo[:ATT_WIDTH], wo[ATT_WIDTH:],
                      norm2_g[l].reshape(1, d), w_mlp_in[l].astype(bf16), w_mlp_out[l].astype(bf16), s)
    return x2.reshape(b, s, d)
```

```python
import functools
import math

import jax
import jax.numpy as jnp
from jax import lax
from jax.experimental import pallas as pl
from jax.experimental.pallas import tpu as pltpu

D_MODEL = 1024
ATT_HEADS = 8
HEAD_DIM = 64
ATT_WIDTH = ATT_HEADS * HEAD_DIM
CONV_CH = D_MODEL - ATT_WIDTH
CONV_WIDTH = 31
D_FF = 4 * D_MODEL
EPS = 1e-6
LOG2E = 1.4426950408889634

LANES = 128
SUBLANES = 8
MXU_TILE = 256
F_PAD = LANES
HALO = 32
NEG_BIG = -1e30

TM_IN = 512
TM_MIX = 512
FF_CHUNK = 512
ATT_BLK = 512
ACC_ROWS = HEAD_DIM + 16
CONV_ROWS = 32
CUM_BLK = 256
VMEM_LIMIT = 56 * 1024 * 1024

bf16 = jnp.bfloat16
f32 = jnp.float32


def _dot(a, b):
    return jnp.dot(a, b, preferred_element_type=f32)


def _split2(y):
    hi = y.astype(bf16)
    lo = (y - hi.astype(f32)).astype(bf16)
    return hi, lo


def _split3(y):
    hi = y.astype(bf16)
    r = y - hi.astype(f32)
    mid = r.astype(bf16)
    lo = (r - mid.astype(f32)).astype(bf16)
    return hi, mid, lo


def _in_proj_kernel(x_ref, g1_ref, wq_ref, wk_ref, wv_ref, wa_ref, wg_ref, wf_ref, bf_ref,
                    gq_ref, gk_ref, gsum_ref, qt_ref, k_ref, vt_ref, h_ref, lf_ref):
    x = x_ref[0]
    ms = jnp.mean(x * x, axis=-1, keepdims=True)
    u = (x * lax.rsqrt(ms + EPS) * g1_ref[...]).astype(bf16)
    gsum = gsum_ref[...]

    def head_norm(y, gain):
        hi, lo = _split2(y * y)
        parts = []
        for c in range(ATT_WIDTH // MXU_TILE):
            sl = slice(c * MXU_TILE, (c + 1) * MXU_TILE)
            parts.append(_dot(hi[:, sl], gsum) + _dot(lo[:, sl], gsum))
        ss = jnp.concatenate(parts, axis=-1)
        return y * lax.rsqrt(ss * (1.0 / HEAD_DIM) + EPS) * gain

    qt_ref[0] = head_norm(_dot(u, wq_ref[...]), gq_ref[...]).T.astype(bf16)
    k_ref[0] = head_norm(_dot(u, wk_ref[...]), gk_ref[...]).astype(bf16)
    vt_ref[0] = _dot(u, wv_ref[...]).T.astype(bf16)
    a = _dot(u, wa_ref[...])
    g = _dot(u, wg_ref[...])
    h_ref[0] = a * jax.nn.sigmoid(g)
    lf_ref[0] = jax.nn.log_sigmoid(_dot(u, wf_ref[...]) + bf_ref[...])


def _in_proj(x3, g1, wq, wk, wv, wa, wg, wf, bfp, gq, gk, gsum):
    b, s, _ = x3.shape
    row = lambda n: pl.BlockSpec((1, TM_IN, n), lambda bi, i: (bi, i, 0))
    col = pl.BlockSpec((1, ATT_WIDTH, TM_IN), lambda bi, i: (bi, 0, i))
    full = lambda a: pl.BlockSpec(a.shape, lambda bi, i: (0,) * a.ndim)
    return pl.pallas_call(
        _in_proj_kernel,
        grid=(b, s // TM_IN),
        in_specs=[row(D_MODEL)] + [full(a) for a in (g1, wq, wk, wv, wa, wg, wf, bfp, gq, gk, gsum)],
        out_specs=[col, row(ATT_WIDTH), col, row(CONV_CH), row(F_PAD)],
        out_shape=[jax.ShapeDtypeStruct((b, ATT_WIDTH, s), bf16), jax.ShapeDtypeStruct((b, s, ATT_WIDTH), bf16),
                   jax.ShapeDtypeStruct((b, ATT_WIDTH, s), bf16),
                   jax.ShapeDtypeStruct((b, s, CONV_CH), f32), jax.ShapeDtypeStruct((b, s, F_PAD), f32)],
        compiler_params=pltpu.CompilerParams(dimension_semantics=("parallel", "parallel"),
                                             vmem_limit_bytes=VMEM_LIMIT),
        name="in_proj",
    )(x3, g1, wq, wk, wv, wa, wg, wf, bfp, gq, gk, gsum)


def _cumsum_kernel(lf_ref, tri_ref, sel_ref, cp_ref):
    s = lf_ref.shape[1]
    tri = tri_ref[...]
    sel = jnp.broadcast_to(sel_ref[...], (CUM_BLK, F_PAD))
    carry = jnp.zeros((1, F_PAD), f32)
    for j in range(s // CUM_BLK):
        rows = slice(j * CUM_BLK, (j + 1) * CUM_BLK)
        hi, mid, lo = _split3(lf_ref[0, rows, :])
        cs = _dot(tri, hi) + _dot(tri, mid) + _dot(tri, lo) + carry
        carry = cs[CUM_BLK - 1:CUM_BLK, :]
        bh, bm, bl = _split3(cs * (-LOG2E))
        piece = jnp.where(sel == 0, bh.astype(f32), jnp.where(sel == 1, bm.astype(f32), bl.astype(f32)))
        cp_ref[0, rows, :] = jnp.where(sel < 3, piece, 0.0).astype(bf16)


def _cumsum(lf3, tri, sel):
    b, s, _ = lf3.shape
    return pl.pallas_call(
        _cumsum_kernel,
        grid=(b,),
        in_specs=[pl.BlockSpec((1, s, F_PAD), lambda i: (i, 0, 0)),
                  pl.BlockSpec(tri.shape, lambda i: (0, 0)),
                  pl.BlockSpec(sel.shape, lambda i: (0, 0))],
        out_specs=pl.BlockSpec((1, s, F_PAD), lambda i: (i, 0, 0)),
        out_shape=jax.ShapeDtypeStruct((b, s, F_PAD), bf16),
        compiler_params=pltpu.CompilerParams(dimension_semantics=("parallel",)),
        name="forget_cumsum",
    )(lf3, tri, sel)


def _attn_kernel(qt_ref, k_ref, cp_ref, vt_ref, mask_ref, o_ref, acc_ref, st_a, st_b, p_a, p_b):
    hp = pl.program_id(1)
    nblk = k_ref.shape[1] // ATT_BLK
    rows = [slice(HEAD_DIM * h, HEAD_DIM * (h + 1)) for h in range(2)]
    row_id = lax.broadcasted_iota(jnp.int32, (LANES, ATT_BLK), 0)
    zeros_t = jnp.zeros((HEAD_DIM, ATT_BLK), bf16)
    ones_t = [jnp.where((row_id >= 3 * (2 * hp + h)) & (row_id < 3 * (2 * hp + h) + 3), 1.0, 0.0).astype(bf16)
              for h in range(2)]

    def blk(i):
        return slice(i * ATT_BLK, (i + 1) * ATT_BLK)

    def scores(qi, kj, st_ref):
        kc = jnp.concatenate([k_ref[0, blk(kj), :], cp_ref[0, blk(kj), :]], axis=1)
        qt = qt_ref[0, :, blk(qi)]
        mx = []
        for h in range(2):
            parts = [qt[rows[0], :], zeros_t] if h == 0 else [zeros_t, qt[rows[1], :]]
            st = _dot(kc, jnp.concatenate(parts + [ones_t[h]], axis=0))
            if qi == kj:
                st = st + mask_ref[...]
            st_ref[h] = st
            mx.append(jnp.max(st, axis=0, keepdims=True))
        return mx

    ones_l = jnp.ones((ACC_ROWS - HEAD_DIM, ATT_BLK), bf16)

    def weighted_values(kj, p_ref):
        return [_dot(jnp.concatenate([vt_ref[0, rows[h], blk(kj)], ones_l], axis=0), p_ref[h]) for h in range(2)]

    pairs = [(qi, kj) for qi in range(nblk) for kj in range(qi + 1)]
    st_bufs, p_bufs = (st_a, st_b), (p_a, p_b)
    mx = scores(*pairs[0], st_bufs[0])
    m_run, alpha = [None, None], [None, None]
    for t in range(len(pairs) + 1):
        if t > 0:
            qi_prev, kj_prev = pairs[t - 1]
            pv_prev = weighted_values(kj_prev, p_bufs[(t - 1) % 2])
            for h in range(2):
                acc_ref[h] = pv_prev[h] if kj_prev == 0 else alpha[h] * acc_ref[h] + pv_prev[h]
            if kj_prev == qi_prev:
                o_t = [acc_ref[h, 0:HEAD_DIM, :] * (1.0 / acc_ref[h, HEAD_DIM:HEAD_DIM + 1, :]) for h in range(2)]
                o_ref[0, blk(qi_prev), :] = jnp.concatenate(o_t, axis=0).T.astype(o_ref.dtype)
        if t == len(pairs):
            break
        qi, kj = pairs[t]
        mx_next = scores(*pairs[t + 1], st_bufs[(t + 1) % 2]) if t + 1 < len(pairs) else None
        for h in range(2):
            if kj == 0:
                m_new = mx[h]
            else:
                m_new = jnp.maximum(m_run[h], mx[h])
                alpha[h] = jnp.exp2(m_run[h] - m_new)
            p_bufs[t % 2][h] = jnp.exp2(st_bufs[t % 2][h] - m_new).astype(bf16)
            m_run[h] = m_new
        mx = mx_next


def _attention(qt3, k3, cp3, vt3, mask):
    b, s, _ = k3.shape
    hp = ATT_WIDTH // LANES
    return pl.pallas_call(
        _attn_kernel,
        grid=(b, hp),
        in_specs=[pl.BlockSpec((1, LANES, s), lambda bi, hi: (bi, hi, 0)),
                  pl.BlockSpec((1, s, LANES), lambda bi, hi: (bi, 0, hi)),
                  pl.BlockSpec((1, s, F_PAD), lambda bi, hi: (bi, 0, 0)),
                  pl.BlockSpec((1, LANES, s), lambda bi, hi: (bi, hi, 0)),
                  pl.BlockSpec(mask.shape, lambda bi, hi: (0, 0))],
        out_specs=pl.BlockSpec((1, s, LANES), lambda bi, hi: (bi, 0, hi)),
        out_shape=jax.ShapeDtypeStruct((b, s, ATT_WIDTH), bf16),
        scratch_shapes=[pltpu.VMEM((2, ACC_ROWS, ATT_BLK), f32),
                        pltpu.VMEM((2, ATT_BLK, ATT_BLK), f32), pltpu.VMEM((2, ATT_BLK, ATT_BLK), f32),
                        pltpu.VMEM((2, ATT_BLK, ATT_BLK), bf16), pltpu.VMEM((2, ATT_BLK, ATT_BLK), bf16)],
        compiler_params=pltpu.CompilerParams(dimension_semantics=("parallel", "parallel"),
                                             vmem_limit_bytes=VMEM_LIMIT),
        name="fox_attention",
    )(qt3, k3, cp3, vt3, mask)


def _conv_module(h_ref, halo, cw_ref, cb_ref, lg_ref, lb_ref, win_ref, sh_ref, hc_ref):
    n_rows = h_ref.shape[0]
    win_ref[0:HALO, :] = halo
    win_ref[HALO:HALO + n_rows, :] = h_ref[...]
    n_sh = sh_ref.shape[1]
    for r in range(1, SUBLANES):
        sh_ref[r] = win_ref[r:r + n_sh, :]
    off = HALO - (CONV_WIDTH - 1)
    for c in range(n_rows // CONV_ROWS):
        acc = None
        for w in range(CONV_WIDTH):
            r = (off + w) % SUBLANES
            base = c * CONV_ROWS + (off + w) - r
            src = win_ref if r == 0 else sh_ref.at[r]
            x = src[base:base + CONV_ROWS, :].reshape(CONV_ROWS // SUBLANES, SUBLANES, -1)
            term = (x * cw_ref[w]).reshape(CONV_ROWS, -1)
            acc = term if acc is None else acc + term
        y = acc + cb_ref[...]
        mu = jnp.mean(y, axis=-1, keepdims=True)
        d = y - mu
        var = jnp.mean(d * d, axis=-1, keepdims=True)
        z = d * lax.rsqrt(var + EPS) * lg_ref[...] + lb_ref[...]
        hc_ref[c * CONV_ROWS:(c + 1) * CONV_ROWS, :] = (z * jax.nn.sigmoid(z)).astype(hc_ref.dtype)


def _mix_mlp_kernel(x_ref, att_ref, h_ref, hh_ref, cw_ref, cb_ref, lg_ref, lb_ref, woa_ref, woc_ref,
                    g2_ref, w1_ref, w2_ref, o_ref, hc_ref, win_ref, sh_ref, acc_ref, *, tiles_per_seq):
    i = pl.program_id(0)
    halo = hh_ref[...]
    halo = jnp.where(i % tiles_per_seq == 0, jnp.zeros_like(halo), halo)
    _conv_module(h_ref, halo, cw_ref, cb_ref, lg_ref, lb_ref, win_ref, sh_ref, hc_ref)

    o_ref[...] = x_ref[...] + _dot(att_ref[...], woa_ref[...]) + _dot(hc_ref[...], woc_ref[...])
    x1 = o_ref[...]
    ms = jnp.mean(x1 * x1, axis=-1, keepdims=True)
    u = (x1 * lax.rsqrt(ms + EPS) * g2_ref[...]).astype(bf16)
    for c in range(D_FF // FF_CHUNK):
        sl = slice(c * FF_CHUNK, (c + 1) * FF_CHUNK)
        h = jnp.maximum(_dot(u, w1_ref[:, sl]), 0.0)
        part = _dot((h * h).astype(bf16), w2_ref[sl, :])
        if c == 0:
            acc_ref[...] = part
        else:
            acc_ref[...] += part
    o_ref[...] += acc_ref[...]


def _mix_mlp(x2, att2, h2, cw8, cb, lg, lb, woa, woc, g2, w1, w2, seq_len):
    t = x2.shape[0]
    ratio = TM_MIX // HALO
    row = lambda w: pl.BlockSpec((TM_MIX, w), lambda i: (i, 0))
    const = lambda a: pl.BlockSpec(a.shape, lambda i: (0,) * a.ndim, pipeline_mode=pl.Buffered(1))
    return pl.pallas_call(
        functools.partial(_mix_mlp_kernel, tiles_per_seq=seq_len // TM_MIX),
        grid=(t // TM_MIX,),
        in_specs=[row(D_MODEL), row(ATT_WIDTH), row(CONV_CH),
                  pl.BlockSpec((HALO, CONV_CH), lambda i: (jnp.maximum(i * ratio - 1, 0), 0)),
                  const(cw8), const(cb), const(lg), const(lb), const(woa), const(woc),
                  const(g2), const(w1), const(w2)],
        out_specs=row(D_MODEL),
        out_shape=jax.ShapeDtypeStruct((t, D_MODEL), f32),
        scratch_shapes=[pltpu.VMEM((TM_MIX, CONV_CH), bf16),
                        pltpu.VMEM((HALO + TM_MIX, CONV_CH), f32),
                        pltpu.VMEM((SUBLANES, HALO + TM_MIX - SUBLANES, CONV_CH), f32),
                        pltpu.VMEM((TM_MIX, D_MODEL), f32)],
        compiler_params=pltpu.CompilerParams(dimension_semantics=("parallel",), vmem_limit_bytes=VMEM_LIMIT),
        name="mix_mlp",
    )(x2, att2, h2, h2, cw8, cb, lg, lb, woa, woc, g2, w1, w2)


def kernel(x, norm1_g, w_in, b_f, q_norm_g, k_norm_g, conv_w, conv_b, conv_ln_g, conv_ln_b, w_o,
           norm2_g, w_mlp_in, w_mlp_out):
    b, s, d = x.shape
    depth = w_in.shape[0]
    o1, o2, o3 = ATT_WIDTH, 2 * ATT_WIDTH, 3 * ATT_WIDTH
    o4 = o3 + ATT_HEADS
    o5 = o4 + CONV_CH

    idx = jnp.arange(MXU_TILE)
    gsum = (idx[:, None] // HEAD_DIM == idx[None, :] // HEAD_DIM).astype(bf16)
    tri = (jnp.arange(CUM_BLK)[:, None] >= jnp.arange(CUM_BLK)[None, :]).astype(bf16)
    lane_ids = jnp.arange(F_PAD, dtype=jnp.int32)
    sel = jnp.where(lane_ids < 3 * ATT_HEADS, lane_ids % 3, 3).reshape(1, F_PAD)
    q_scale = LOG2E / math.sqrt(HEAD_DIM)
    blk_ids = jnp.arange(ATT_BLK)
    mask = jnp.where(blk_ids[:, None] > blk_ids[None, :], NEG_BIG, 0.0).astype(f32)

    x2 = x.reshape(b * s, d)
    for l in range(depth):
        wl = w_in[l]
        wq, wk, wv = (wl[:, :o1].astype(bf16), wl[:, o1:o2].astype(bf16), wl[:, o2:o3].astype(bf16))
        wf = jnp.pad(jnp.repeat(wl[:, o3:o4], 3, axis=1), ((0, 0), (0, F_PAD - 3 * ATT_HEADS))).astype(bf16)
        wa, wg = wl[:, o4:o5].astype(bf16), wl[:, o5:].astype(bf16)
        bfp = jnp.pad(jnp.repeat(b_f[l], 3), (0, F_PAD - 3 * ATT_HEADS)).reshape(1, F_PAD)
        gq = (jnp.tile(q_norm_g[l], ATT_HEADS) * q_scale).reshape(1, ATT_WIDTH)
        gk = jnp.tile(k_norm_g[l], ATT_HEADS).reshape(1, ATT_WIDTH)

        qt3, k3, vt3, h3, lf3 = _in_proj(x2.reshape(b, s, d), norm1_g[l].reshape(1, d), wq, wk, wv, wa, wg, wf,
                                        bfp, gq, gk, gsum)
        cp3 = _cumsum(lf3, tri, sel)
        att = _attention(qt3, k3, cp3, vt3, mask)
        cw8 = jnp.broadcast_to(conv_w[l][:, None, :], (CONV_WIDTH, SUBLANES, CONV_CH))
        wo = w_o[l].astype(bf16)
        x2 = _mix_mlp(x2, att.reshape(b * s, ATT_WIDTH), h3.reshape(b * s, CONV_CH), cw8,
                      conv_b[l].reshape(1, CONV_CH), conv_ln_g[l].reshape(1, CONV_CH),
                      conv_ln_b[l].reshape(1, CONV_CH), wo[:ATT_WIDTH], wo[ATT_WIDTH:],
                      norm2_g[l].reshape(1, d), w_mlp_in[l].astype(bf16), w_mlp_out[l].astype(bf16), s)
    return x2.reshape(b, s, d)
```

```python
import functools
import math

import jax
import jax.numpy as jnp
from jax import lax
from jax.experimental import pallas as pl
from jax.experimental.pallas import tpu as pltpu

D_MODEL = 1024
ATT_HEADS = 8
HEAD_DIM = 64
ATT_WIDTH = ATT_HEADS * HEAD_DIM
CONV_CH = D_MODEL - ATT_WIDTH
CONV_WIDTH = 31
D_FF = 4 * D_MODEL
EPS = 1e-6
LOG2E = 1.4426950408889634

LANES = 128
SUBLANES = 8
MXU_TILE = 256
F_PAD = LANES
HALO = 32
NEG_BIG = -1e30

TM_IN = 512
TM_MIX = 512
FF_CHUNK = 512
ATT_BLK = 512
ACC_ROWS = HEAD_DIM + 16
CONV_ROWS = 32
CUM_BLK = 256
VMEM_LIMIT = 56 * 1024 * 1024

bf16 = jnp.bfloat16
f32 = jnp.float32


def _dot(a, b):
    return jnp.dot(a, b, preferred_element_type=f32)


def _split2(y):
    hi = y.astype(bf16)
    lo = (y - hi.astype(f32)).astype(bf16)
    return hi, lo


def _split3(y):
    hi = y.astype(bf16)
    r = y - hi.astype(f32)
    mid = r.astype(bf16)
    lo = (r - mid.astype(f32)).astype(bf16)
    return hi, mid, lo


def _in_proj_kernel(x_ref, g1_ref, wq_ref, wk_ref, wv_ref, wa_ref, wg_ref, wf_ref, bf_ref,
                    gq_ref, gk_ref, gsum_ref, qt_ref, k_ref, vt_ref, h_ref, lf_ref):
    x = x_ref[0]
    ms = jnp.mean(x * x, axis=-1, keepdims=True)
    u = (x * lax.rsqrt(ms + EPS) * g1_ref[...]).astype(bf16)
    gsum = gsum_ref[...]

    def head_norm(y, gain):
        hi, lo = _split2(y * y)
        parts = []
        for c in range(ATT_WIDTH // MXU_TILE):
            sl = slice(c * MXU_TILE, (c + 1) * MXU_TILE)
            parts.append(_dot(hi[:, sl], gsum) + _dot(lo[:, sl], gsum))
        ss = jnp.concatenate(parts, axis=-1)
        return y * lax.rsqrt(ss * (1.0 / HEAD_DIM) + EPS) * gain

    qt_ref[0] = head_norm(_dot(u, wq_ref[...]), gq_ref[...]).T.astype(bf16)
    k_ref[0] = head_norm(_dot(u, wk_ref[...]), gk_ref[...]).astype(bf16)
    vt_ref[0] = _dot(u, wv_ref[...]).T.astype(bf16)
    a = _dot(u, wa_ref[...])
    g = _dot(u, wg_ref[...])
    h_ref[0] = a * jax.nn.sigmoid(g)
    lf_ref[0] = jax.nn.log_sigmoid(_dot(u, wf_ref[...]) + bf_ref[...])


def _in_proj(x3, g1, wq, wk, wv, wa, wg, wf, bfp, gq, gk, gsum):
    b, s, _ = x3.shape
    row = lambda n: pl.BlockSpec((1, TM_IN, n), lambda bi, i: (bi, i, 0))
    col = pl.BlockSpec((1, ATT_WIDTH, TM_IN), lambda bi, i: (bi, 0, i))
    full = lambda a: pl.BlockSpec(a.shape, lambda bi, i: (0,) * a.ndim)
    return pl.pallas_call(
        _in_proj_kernel,
        grid=(b, s // TM_IN),
        in_specs=[row(D_MODEL)] + [full(a) for a in (g1, wq, wk, wv, wa, wg, wf, bfp, gq, gk, gsum)],
        out_specs=[col, row(ATT_WIDTH), col, row(CONV_CH), row(F_PAD)],
        out_shape=[jax.ShapeDtypeStruct((b, ATT_WIDTH, s), bf16), jax.ShapeDtypeStruct((b, s, ATT_WIDTH), bf16),
                   jax.ShapeDtypeStruct((b, ATT_WIDTH, s), bf16),
                   jax.ShapeDtypeStruct((b, s, CONV_CH), f32), jax.ShapeDtypeStruct((b, s, F_PAD), f32)],
        compiler_params=pltpu.CompilerParams(dimension_semantics=("parallel", "parallel"),
                                             vmem_limit_bytes=VMEM_LIMIT),
        name="in_proj",
    )(x3, g1, wq, wk, wv, wa, wg, wf, bfp, gq, gk, gsum)


def _cumsum_kernel(lf_ref, tri_ref, sel_ref, cp_ref):
    s = lf_ref.shape[1]
    tri = tri_ref[...]
    sel = jnp.broadcast_to(sel_ref[...], (CUM_BLK, F_PAD))
    carry = jnp.zeros((1, F_PAD), f32)
    for j in range(s // CUM_BLK):
        rows = slice(j * CUM_BLK, (j + 1) * CUM_BLK)
        hi, mid, lo = _split3(lf_ref[0, rows, :])
        cs = _dot(tri, hi) + _dot(tri, mid) + _dot(tri, lo) + carry
        carry = cs[CUM_BLK - 1:CUM_BLK, :]
        bh, bm, bl = _split3(cs * (-LOG2E))
        piece = jnp.where(sel == 0, bh.astype(f32), jnp.where(sel == 1, bm.astype(f32), bl.astype(f32)))
        cp_ref[0, rows, :] = jnp.where(sel < 3, piece, 0.0).astype(bf16)


def _cumsum(lf3, tri, sel):
    b, s, _ = lf3.shape
    return pl.pallas_call(
        _cumsum_kernel,
        grid=(b,),
        in_specs=[pl.BlockSpec((1, s, F_PAD), lambda i: (i, 0, 0)),
                  pl.BlockSpec(tri.shape, lambda i: (0, 0)),
                  pl.BlockSpec(sel.shape, lambda i: (0, 0))],
        out_specs=pl.BlockSpec((1, s, F_PAD), lambda i: (i, 0, 0)),
        out_shape=jax.ShapeDtypeStruct((b, s, F_PAD), bf16),
        compiler_params=pltpu.CompilerParams(dimension_semantics=("parallel",)),
        name="forget_cumsum",
    )(lf3, tri, sel)


def _attn_kernel(qt_ref, k_ref, cp_ref, vt_ref, mask_ref, slot_ref, o_ref, acc_ref, st_a, st_b):
    hp = pl.program_id(1)
    nblk = k_ref.shape[1] // ATT_BLK
    rows = [slice(HEAD_DIM * h, HEAD_DIM * (h + 1)) for h in range(2)]
    row_id = lax.broadcasted_iota(jnp.int32, (LANES, ATT_BLK), 0)
    zeros_t = jnp.zeros((HEAD_DIM, ATT_BLK), bf16)
    ones_t = [jnp.where((row_id >= 3 * (2 * hp + h)) & (row_id < 3 * (2 * hp + h) + 3), 1.0, 0.0).astype(bf16)
              for h in range(2)]

    def blk(i):
        return slice(i * ATT_BLK, (i + 1) * ATT_BLK)

    def scores(qi, kj, st_ref):
        kc = jnp.concatenate([k_ref[0, blk(kj), :], cp_ref[0, blk(kj), :]], axis=1)
        qt = qt_ref[0, :, blk(qi)]
        mx = []
        for h in range(2):
            parts = [qt[rows[0], :], zeros_t] if h == 0 else [zeros_t, qt[rows[1], :]]
            st = _dot(kc, jnp.concatenate(parts + [ones_t[h]], axis=0))
            if qi == kj:
                st = st + mask_ref[...]
            st_ref[h, slot] = st
            mx.append(jnp.max(st, axis=0, keepdims=True))
        return mx

    ones_l = jnp.ones((ACC_ROWS - HEAD_DIM, ATT_BLK), bf16)

    def fold(pair, alpha, pv):
        qi, kj = pair
        for h in range(2):
            acc_ref[h] = pv[h] if kj == 0 else alpha[h] * acc_ref[h] + pv[h]
        if kj == qi:
            o_t = [acc_ref[h, 0:HEAD_DIM, :] * (1.0 / acc_ref[h, HEAD_DIM:HEAD_DIM + 1, :]) for h in range(2)]
            o_ref[0, blk(qi), :] = jnp.concatenate(o_t, axis=0).T.astype(o_ref.dtype)

    pairs = [(qi, kj) for qi in range(nblk) for kj in range(qi + 1)]
    st_bufs = (st_a, st_b)
    slot = slot_ref[0]
    mx = scores(*pairs[0], st_bufs[0])
    m_run = [None, None]
    pending = None
    for t, (qi, kj) in enumerate(pairs):
        mx_next = scores(*pairs[t + 1], st_bufs[(t + 1) % 2]) if t + 1 < len(pairs) else None
        alpha, p = [None, None], [None, None]
        for h in range(2):
            if kj == 0:
                m_new = mx[h]
            else:
                m_new = jnp.maximum(m_run[h], mx[h])
                alpha[h] = jnp.exp2(m_run[h] - m_new)
            p[h] = jnp.exp2(st_bufs[t % 2][h, slot] - m_new).astype(bf16)
            m_run[h] = m_new
        if pending is not None:
            fold(*pending)
        pv = [_dot(jnp.concatenate([vt_ref[0, rows[h], blk(kj)], ones_l], axis=0), p[h]) for h in range(2)]
        pending = ((qi, kj), alpha, pv)
        mx = mx_next
    fold(*pending)


def _attention(qt3, k3, cp3, vt3, mask):
    b, s, _ = k3.shape
    hp = ATT_WIDTH // LANES
    return pl.pallas_call(
        _attn_kernel,
        grid=(b, hp),
        in_specs=[pl.BlockSpec((1, LANES, s), lambda bi, hi: (bi, hi, 0)),
                  pl.BlockSpec((1, s, LANES), lambda bi, hi: (bi, 0, hi)),
                  pl.BlockSpec((1, s, F_PAD), lambda bi, hi: (bi, 0, 0)),
                  pl.BlockSpec((1, LANES, s), lambda bi, hi: (bi, hi, 0)),
                  pl.BlockSpec(mask.shape, lambda bi, hi: (0, 0)),
                  pl.BlockSpec(memory_space=pltpu.SMEM)],
        out_specs=pl.BlockSpec((1, s, LANES), lambda bi, hi: (bi, 0, hi)),
        out_shape=jax.ShapeDtypeStruct((b, s, ATT_WIDTH), bf16),
        scratch_shapes=[pltpu.VMEM((2, ACC_ROWS, ATT_BLK), f32),
                        pltpu.VMEM((2, 2, ATT_BLK, ATT_BLK), f32), pltpu.VMEM((2, 2, ATT_BLK, ATT_BLK), f32)],
        compiler_params=pltpu.CompilerParams(dimension_semantics=("parallel", "parallel"),
                                             vmem_limit_bytes=VMEM_LIMIT),
        name="fox_attention",
    )(qt3, k3, cp3, vt3, mask, jnp.zeros((1,), jnp.int32))


def _conv_module(h_ref, halo, cw_ref, cb_ref, lg_ref, lb_ref, win_ref, sh_ref, hc_ref):
    n_rows = h_ref.shape[0]
    win_ref[0:HALO, :] = halo
    win_ref[HALO:HALO + n_rows, :] = h_ref[...]
    n_sh = sh_ref.shape[1]
    for r in range(1, SUBLANES):
        sh_ref[r] = win_ref[r:r + n_sh, :]
    off = HALO - (CONV_WIDTH - 1)
    for c in range(n_rows // CONV_ROWS):
        acc = None
        for w in range(CONV_WIDTH):
            r = (off + w) % SUBLANES
            base = c * CONV_ROWS + (off + w) - r
            src = win_ref if r == 0 else sh_ref.at[r]
            x = src[base:base + CONV_ROWS, :].reshape(CONV_ROWS // SUBLANES, SUBLANES, -1)
            term = (x * cw_ref[w]).reshape(CONV_ROWS, -1)
            acc = term if acc is None else acc + term
        y = acc + cb_ref[...]
        mu = jnp.mean(y, axis=-1, keepdims=True)
        d = y - mu
        var = jnp.mean(d * d, axis=-1, keepdims=True)
        z = d * lax.rsqrt(var + EPS) * lg_ref[...] + lb_ref[...]
        hc_ref[c * CONV_ROWS:(c + 1) * CONV_ROWS, :] = (z * jax.nn.sigmoid(z)).astype(hc_ref.dtype)


def _mix_mlp_kernel(x_ref, att_ref, h_ref, hh_ref, cw_ref, cb_ref, lg_ref, lb_ref, woa_ref, woc_ref,
                    g2_ref, w1_ref, w2_ref, o_ref, hc_ref, win_ref, sh_ref, acc_ref, *, tiles_per_seq):
    i = pl.program_id(0)
    halo = hh_ref[...]
    halo = jnp.where(i % tiles_per_seq == 0, jnp.zeros_like(halo), halo)
    _conv_module(h_ref, halo, cw_ref, cb_ref, lg_ref, lb_ref, win_ref, sh_ref, hc_ref)

    o_ref[...] = x_ref[...] + _dot(att_ref[...], woa_ref[...]) + _dot(hc_ref[...], woc_ref[...])
    x1 = o_ref[...]
    ms = jnp.mean(x1 * x1, axis=-1, keepdims=True)
    u = (x1 * lax.rsqrt(ms + EPS) * g2_ref[...]).astype(bf16)
    for c in range(D_FF // FF_CHUNK):
        sl = slice(c * FF_CHUNK, (c + 1) * FF_CHUNK)
        h = jnp.maximum(_dot(u, w1_ref[:, sl]), 0.0)
        part = _dot((h * h).astype(bf16), w2_ref[sl, :])
        if c == 0:
            acc_ref[...] = part
        else:
            acc_ref[...] += part
    o_ref[...] += acc_ref[...]


def _mix_mlp(x2, att2, h2, cw8, cb, lg, lb, woa, woc, g2, w1, w2, seq_len):
    t = x2.shape[0]
    ratio = TM_MIX // HALO
    row = lambda w: pl.BlockSpec((TM_MIX, w), lambda i: (i, 0))
    const = lambda a: pl.BlockSpec(a.shape, lambda i: (0,) * a.ndim, pipeline_mode=pl.Buffered(1))
    return pl.pallas_call(
        functools.partial(_mix_mlp_kernel, tiles_per_seq=seq_len // TM_MIX),
        grid=(t // TM_MIX,),
        in_specs=[row(D_MODEL), row(ATT_WIDTH), row(CONV_CH),
                  pl.BlockSpec((HALO, CONV_CH), lambda i: (jnp.maximum(i * ratio - 1, 0), 0)),
                  const(cw8), const(cb), const(lg), const(lb), const(woa), const(woc),
                  const(g2), const(w1), const(w2)],
        out_specs=row(D_MODEL),
        out_shape=jax.ShapeDtypeStruct((t, D_MODEL), f32),
        scratch_shapes=[pltpu.VMEM((TM_MIX, CONV_CH), bf16),
                        pltpu.VMEM((HALO + TM_MIX, CONV_CH), f32),
                        pltpu.VMEM((SUBLANES, HALO + TM_MIX - SUBLANES, CONV_CH), f32),
                        pltpu.VMEM((TM_MIX, D_MODEL), f32)],
        compiler_params=pltpu.CompilerParams(dimension_semantics=("parallel",), vmem_limit_bytes=VMEM_LIMIT),
        name="mix_mlp",
    )(x2, att2, h2, h2, cw8, cb, lg, lb, woa, woc, g2, w1, w2)


def kernel(x, norm1_g, w_in, b_f, q_norm_g, k_norm_g, conv_w, conv_b, conv_ln_g, conv_ln_b, w_o,
           norm2_g, w_mlp_in, w_mlp_out):
    b, s, d = x.shape
    depth = w_in.shape[0]
    o1, o2, o3 = ATT_WIDTH, 2 * ATT_WIDTH, 3 * ATT_WIDTH
    o4 = o3 + ATT_HEADS
    o5 = o4 + CONV_CH

    idx = jnp.arange(MXU_TILE)
    gsum = (idx[:, None] // HEAD_DIM == idx[None, :] // HEAD_DIM).astype(bf16)
    tri = (jnp.arange(CUM_BLK)[:, None] >= jnp.arange(CUM_BLK)[None, :]).astype(bf16)
    lane_ids = jnp.arange(F_PAD, dtype=jnp.int32)
    sel = jnp.where(lane_ids < 3 * ATT_HEADS, lane_ids % 3, 3).reshape(1, F_PAD)
    q_scale = LOG2E / math.sqrt(HEAD_DIM)
    blk_ids = jnp.arange(ATT_BLK)
    mask = jnp.where(blk_ids[:, None] > blk_ids[None, :], NEG_BIG, 0.0).astype(f32)

    x2 = x.reshape(b * s, d)
    for l in range(depth):
        wl = w_in[l]
        wq, wk, wv = (wl[:, :o1].astype(bf16), wl[:, o1:o2].astype(bf16), wl[:, o2:o3].astype(bf16))
        wf = jnp.pad(jnp.repeat(wl[:, o3:o4], 3, axis=1), ((0, 0), (0, F_PAD - 3 * ATT_HEADS))).astype(bf16)
        wa, wg = wl[:, o4:o5].astype(bf16), wl[:, o5:].astype(bf16)
        bfp = jnp.pad(jnp.repeat(b_f[l], 3), (0, F_PAD - 3 * ATT_HEADS)).reshape(1, F_PAD)
        gq = (jnp.tile(q_norm_g[l], ATT_HEADS) * q_scale).reshape(1, ATT_WIDTH)
        gk = jnp.tile(k_norm_g[l], ATT_HEADS).reshape(1, ATT_WIDTH)

        qt3, k3, vt3, h3, lf3 = _in_proj(x2.reshape(b, s, d), norm1_g[l].reshape(1, d), wq, wk, wv, wa, wg, wf,
                                        bfp, gq, gk, gsum)
        cp3 = _cumsum(lf3, tri, sel)
        att = _attention(qt3, k3, cp3, vt3, mask)
        cw8 = jnp.broadcast_to(conv_w[l][:, None, :], (CONV_WIDTH, SUBLANES, CONV_CH))
        wo = w_o[l].astype(bf16)
        x2 = _mix_mlp(x2, att.reshape(b * s, ATT_WIDTH), h3.reshape(b * s, CONV_CH), cw8,
                      conv_b[l].reshape(1, CONV_CH), conv_ln_g[l].reshape(1, CONV_CH),
                      conv_ln_b[l].reshape(1, CONV_CH), wo[:ATT_WIDTH], wo[ATT_WIDTH:],
                      norm2_g[l].reshape(1, d), w_mlp_in[l].astype(bf16), w_mlp_out[l].astype(bf16), s)
    return x2.reshape(b, s, d)
```

```python
import functools
import math

import jax
import jax.numpy as jnp
from jax import lax
from jax.experimental import pallas as pl
from jax.experimental.pallas import tpu as pltpu

D_MODEL = 1024
ATT_HEADS = 8
HEAD_DIM = 64
ATT_WIDTH = ATT_HEADS * HEAD_DIM
CONV_CH = D_MODEL - ATT_WIDTH
CONV_WIDTH = 31
D_FF = 4 * D_MODEL
EPS = 1e-6
LOG2E = 1.4426950408889634

LANES = 128
SUBLANES = 8
F_PAD = LANES
HALO = 32
NEG_BIG = -1e30

TM_IN = 512
TM_MIX = 512
FF_CHUNK = 512
ATT_BLK = 512
ACC_ROWS = HEAD_DIM + 16
CONV_ROWS = 32
CUM_BLK = 256
VMEM_LIMIT = 56 * 1024 * 1024

bf16 = jnp.bfloat16
f32 = jnp.float32


def _dot(a, b):
    return jnp.dot(a, b, preferred_element_type=f32)


def _split3(y):
    hi = y.astype(bf16)
    r = y - hi.astype(f32)
    mid = r.astype(bf16)
    lo = (r - mid.astype(f32)).astype(bf16)
    return hi, mid, lo


def _in_proj_kernel(x_ref, g1_ref, wq_ref, wk_ref, wv_ref, wa_ref, wg_ref, wf_ref, bf_ref,
                    gq_ref, gk_ref, qt_ref, k_ref, vt_ref, h_ref, lf_ref):
    x = x_ref[0]
    ms = jnp.mean(x * x, axis=-1, keepdims=True)
    u = (x * lax.rsqrt(ms + EPS) * g1_ref[...]).astype(bf16)

    def head_norm_t(y, gain_ref):
        yt = y.T
        out = []
        for h in range(ATT_HEADS):
            yh = yt[h * HEAD_DIM:(h + 1) * HEAD_DIM, :]
            ss = jnp.sum(yh * yh, axis=0, keepdims=True)
            out.append(yh * lax.rsqrt(ss * (1.0 / HEAD_DIM) + EPS))
        gain = jnp.concatenate([gain_ref[...]] * (TM_IN // LANES), axis=1)
        return jnp.concatenate(out, axis=0) * gain

    qt_ref[0] = head_norm_t(_dot(u, wq_ref[...]), gq_ref).astype(bf16)
    k_ref[0] = head_norm_t(_dot(u, wk_ref[...]), gk_ref).T.astype(bf16)
    vt_ref[0] = _dot(u, wv_ref[...]).T.astype(bf16)
    a = _dot(u, wa_ref[...])
    g = _dot(u, wg_ref[...])
    h_ref[0] = a * jax.nn.sigmoid(g)
    lf_ref[0] = jax.nn.log_sigmoid(_dot(u, wf_ref[...]) + bf_ref[...])


def _in_proj(x3, g1, wq, wk, wv, wa, wg, wf, bfp, gq, gk):
    b, s, _ = x3.shape
    row = lambda n: pl.BlockSpec((1, TM_IN, n), lambda bi, i: (bi, i, 0))
    col = pl.BlockSpec((1, ATT_WIDTH, TM_IN), lambda bi, i: (bi, 0, i))
    full = lambda a: pl.BlockSpec(a.shape, lambda bi, i: (0,) * a.ndim)
    return pl.pallas_call(
        _in_proj_kernel,
        grid=(b, s // TM_IN),
        in_specs=[row(D_MODEL)] + [full(a) for a in (g1, wq, wk, wv, wa, wg, wf, bfp, gq, gk)],
        out_specs=[col, row(ATT_WIDTH), col, row(CONV_CH), row(F_PAD)],
        out_shape=[jax.ShapeDtypeStruct((b, ATT_WIDTH, s), bf16), jax.ShapeDtypeStruct((b, s, ATT_WIDTH), bf16),
                   jax.ShapeDtypeStruct((b, ATT_WIDTH, s), bf16),
                   jax.ShapeDtypeStruct((b, s, CONV_CH), f32), jax.ShapeDtypeStruct((b, s, F_PAD), f32)],
        compiler_params=pltpu.CompilerParams(dimension_semantics=("parallel", "parallel"),
                                             vmem_limit_bytes=VMEM_LIMIT),
        name="in_proj",
    )(x3, g1, wq, wk, wv, wa, wg, wf, bfp, gq, gk)


def _cumsum_kernel(lf_ref, tri_ref, sel_ref, cp_ref):
    s = lf_ref.shape[1]
    tri = tri_ref[...]
    sel = jnp.broadcast_to(sel_ref[...], (CUM_BLK, F_PAD))
    carry = jnp.zeros((1, F_PAD), f32)
    for j in range(s // CUM_BLK):
        rows = slice(j * CUM_BLK, (j + 1) * CUM_BLK)
        hi, mid, lo = _split3(lf_ref[0, rows, :])
        cs = _dot(tri, hi) + _dot(tri, mid) + _dot(tri, lo) + carry
        carry = cs[CUM_BLK - 1:CUM_BLK, :]
        bh, bm, bl = _split3(cs * (-LOG2E))
        piece = jnp.where(sel == 0, bh.astype(f32), jnp.where(sel == 1, bm.astype(f32), bl.astype(f32)))
        cp_ref[0, rows, :] = jnp.where(sel < 3, piece, 0.0).astype(bf16)


def _cumsum(lf3, tri, sel):
    b, s, _ = lf3.shape
    return pl.pallas_call(
        _cumsum_kernel,
        grid=(b,),
        in_specs=[pl.BlockSpec((1, s, F_PAD), lambda i: (i, 0, 0)),
                  pl.BlockSpec(tri.shape, lambda i: (0, 0)),
                  pl.BlockSpec(sel.shape, lambda i: (0, 0))],
        out_specs=pl.BlockSpec((1, s, F_PAD), lambda i: (i, 0, 0)),
        out_shape=jax.ShapeDtypeStruct((b, s, F_PAD), bf16),
        compiler_params=pltpu.CompilerParams(dimension_semantics=("parallel",)),
        name="forget_cumsum",
    )(lf3, tri, sel)


def _attn_kernel(qt_ref, k_ref, cp_ref, vt_ref, mask_ref, slot_ref, o_ref, acc_ref, st_a, st_b):
    hp = pl.program_id(1)
    nblk = k_ref.shape[1] // ATT_BLK
    rows = [slice(HEAD_DIM * h, HEAD_DIM * (h + 1)) for h in range(2)]
    row_id = lax.broadcasted_iota(jnp.int32, (LANES, ATT_BLK), 0)
    zeros_t = jnp.zeros((HEAD_DIM, ATT_BLK), bf16)
    ones_t = [jnp.where((row_id >= 3 * (2 * hp + h)) & (row_id < 3 * (2 * hp + h) + 3), 1.0, 0.0).astype(bf16)
              for h in range(2)]

    def blk(i):
        return slice(i * ATT_BLK, (i + 1) * ATT_BLK)

    def scores(qi, kj, st_ref):
        kc = jnp.concatenate([k_ref[0, blk(kj), :], cp_ref[0, blk(kj), :]], axis=1)
        qt = qt_ref[0, :, blk(qi)]
        mx = []
        for h in range(2):
            parts = [qt[rows[0], :], zeros_t] if h == 0 else [zeros_t, qt[rows[1], :]]
            st = _dot(kc, jnp.concatenate(parts + [ones_t[h]], axis=0))
            if qi == kj:
                st = st + mask_ref[...]
            st_ref[h, slot] = st
            mx.append(jnp.max(st, axis=0, keepdims=True))
        return mx

    ones_l = jnp.ones((ACC_ROWS - HEAD_DIM, ATT_BLK), bf16)

    def fold(pair, alpha, pv):
        qi, kj = pair
        for h in range(2):
            acc_ref[h] = pv[h] if kj == 0 else alpha[h] * acc_ref[h] + pv[h]
        if kj == qi:
            o_t = [acc_ref[h, 0:HEAD_DIM, :] * (1.0 / acc_ref[h, HEAD_DIM:HEAD_DIM + 1, :]) for h in range(2)]
            o_ref[0, blk(qi), :] = jnp.concatenate(o_t, axis=0).T.astype(o_ref.dtype)

    pairs = [(qi, kj) for qi in range(nblk) for kj in range(qi + 1)]
    st_bufs = (st_a, st_b)
    slot = slot_ref[0]
    mx = scores(*pairs[0], st_bufs[0])
    m_run = [None, None]
    pending = None
    for t, (qi, kj) in enumerate(pairs):
        mx_next = scores(*pairs[t + 1], st_bufs[(t + 1) % 2]) if t + 1 < len(pairs) else None
        alpha, p = [None, None], [None, None]
        for h in range(2):
            if kj == 0:
                m_new = mx[h]
            else:
                m_new = jnp.maximum(m_run[h], mx[h])
                alpha[h] = jnp.exp2(m_run[h] - m_new)
            p[h] = jnp.exp2(st_bufs[t % 2][h, slot] - m_new).astype(bf16)
            m_run[h] = m_new
        if pending is not None:
            fold(*pending)
        pv = [_dot(jnp.concatenate([vt_ref[0, rows[h], blk(kj)], ones_l], axis=0), p[h]) for h in range(2)]
        pending = ((qi, kj), alpha, pv)
        mx = mx_next
    fold(*pending)


def _attention(qt3, k3, cp3, vt3, mask):
    b, s, _ = k3.shape
    hp = ATT_WIDTH // LANES
    return pl.pallas_call(
        _attn_kernel,
        grid=(b, hp),
        in_specs=[pl.BlockSpec((1, LANES, s), lambda bi, hi: (bi, hi, 0)),
                  pl.BlockSpec((1, s, LANES), lambda bi, hi: (bi, 0, hi)),
                  pl.BlockSpec((1, s, F_PAD), lambda bi, hi: (bi, 0, 0)),
                  pl.BlockSpec((1, LANES, s), lambda bi, hi: (bi, hi, 0)),
                  pl.BlockSpec(mask.shape, lambda bi, hi: (0, 0)),
                  pl.BlockSpec(memory_space=pltpu.SMEM)],
        out_specs=pl.BlockSpec((1, s, LANES), lambda bi, hi: (bi, 0, hi)),
        out_shape=jax.ShapeDtypeStruct((b, s, ATT_WIDTH), bf16),
        scratch_shapes=[pltpu.VMEM((2, ACC_ROWS, ATT_BLK), f32),
                        pltpu.VMEM((2, 2, ATT_BLK, ATT_BLK), f32), pltpu.VMEM((2, 2, ATT_BLK, ATT_BLK), f32)],
        compiler_params=pltpu.CompilerParams(dimension_semantics=("parallel", "parallel"),
                                             vmem_limit_bytes=VMEM_LIMIT),
        name="fox_attention",
    )(qt3, k3, cp3, vt3, mask, jnp.zeros((1,), jnp.int32))


def _conv_module(h_ref, halo, cw_ref, cb_ref, lg_ref, lb_ref, win_ref, sh_ref, hc_ref):
    n_rows = h_ref.shape[0]
    win_ref[0:HALO, :] = halo
    win_ref[HALO:HALO + n_rows, :] = h_ref[...]
    n_sh = sh_ref.shape[1]
    for r in range(1, SUBLANES):
        sh_ref[r] = win_ref[r:r + n_sh, :]
    off = HALO - (CONV_WIDTH - 1)
    for c in range(n_rows // CONV_ROWS):
        acc = None
        for w in range(CONV_WIDTH):
            r = (off + w) % SUBLANES
            base = c * CONV_ROWS + (off + w) - r
            src = win_ref if r == 0 else sh_ref.at[r]
            x = src[base:base + CONV_ROWS, :].reshape(CONV_ROWS // SUBLANES, SUBLANES, -1)
            term = (x * cw_ref[w]).reshape(CONV_ROWS, -1)
            acc = term if acc is None else acc + term
        y = acc + cb_ref[...]
        mu = jnp.mean(y, axis=-1, keepdims=True)
        d = y - mu
        var = jnp.mean(d * d, axis=-1, keepdims=True)
        z = d * lax.rsqrt(var + EPS) * lg_ref[...] + lb_ref[...]
        hc_ref[c * CONV_ROWS:(c + 1) * CONV_ROWS, :] = (z * jax.nn.sigmoid(z)).astype(hc_ref.dtype)


def _mix_mlp_kernel(x_ref, att_ref, h_ref, hh_ref, cw_ref, cb_ref, lg_ref, lb_ref, woa_ref, woc_ref,
                    g2_ref, w1_ref, w2_ref, o_ref, hc_ref, win_ref, sh_ref, acc_ref, *, tiles_per_seq):
    i = pl.program_id(0)
    halo = hh_ref[...]
    halo = jnp.where(i % tiles_per_seq == 0, jnp.zeros_like(halo), halo)
    _conv_module(h_ref, halo, cw_ref, cb_ref, lg_ref, lb_ref, win_ref, sh_ref, hc_ref)

    o_ref[...] = x_ref[...] + _dot(att_ref[...], woa_ref[...]) + _dot(hc_ref[...], woc_ref[...])
    x1 = o_ref[...]
    ms = jnp.mean(x1 * x1, axis=-1, keepdims=True)
    u = (x1 * lax.rsqrt(ms + EPS) * g2_ref[...]).astype(bf16)
    for c in range(D_FF // FF_CHUNK):
        sl = slice(c * FF_CHUNK, (c + 1) * FF_CHUNK)
        h = jnp.maximum(_dot(u, w1_ref[:, sl]), 0.0)
        part = _dot((h * h).astype(bf16), w2_ref[sl, :])
        if c == 0:
            acc_ref[...] = part
        else:
            acc_ref[...] += part
    o_ref[...] += acc_ref[...]


def _mix_mlp(x2, att2, h2, cw8, cb, lg, lb, woa, woc, g2, w1, w2, seq_len):
    t = x2.shape[0]
    ratio = TM_MIX // HALO
    row = lambda w: pl.BlockSpec((TM_MIX, w), lambda i: (i, 0))
    const = lambda a: pl.BlockSpec(a.shape, lambda i: (0,) * a.ndim, pipeline_mode=pl.Buffered(1))
    return pl.pallas_call(
        functools.partial(_mix_mlp_kernel, tiles_per_seq=seq_len // TM_MIX),
        grid=(t // TM_MIX,),
        in_specs=[row(D_MODEL), row(ATT_WIDTH), row(CONV_CH),
                  pl.BlockSpec((HALO, CONV_CH), lambda i: (jnp.maximum(i * ratio - 1, 0), 0)),
                  const(cw8), const(cb), const(lg), const(lb), const(woa), const(woc),
                  const(g2), const(w1), const(w2)],
        out_specs=row(D_MODEL),
        out_shape=jax.ShapeDtypeStruct((t, D_MODEL), f32),
        scratch_shapes=[pltpu.VMEM((TM_MIX, CONV_CH), bf16),
                        pltpu.VMEM((HALO + TM_MIX, CONV_CH), f32),
                        pltpu.VMEM((SUBLANES, HALO + TM_MIX - SUBLANES, CONV_CH), f32),
                        pltpu.VMEM((TM_MIX, D_MODEL), f32)],
        compiler_params=pltpu.CompilerParams(dimension_semantics=("parallel",), vmem_limit_bytes=VMEM_LIMIT),
        name="mix_mlp",
    )(x2, att2, h2, h2, cw8, cb, lg, lb, woa, woc, g2, w1, w2)


def kernel(x, norm1_g, w_in, b_f, q_norm_g, k_norm_g, conv_w, conv_b, conv_ln_g, conv_ln_b, w_o,
           norm2_g, w_mlp_in, w_mlp_out):
    b, s, d = x.shape
    depth = w_in.shape[0]
    o1, o2, o3 = ATT_WIDTH, 2 * ATT_WIDTH, 3 * ATT_WIDTH
    o4 = o3 + ATT_HEADS
    o5 = o4 + CONV_CH

    tri = (jnp.arange(CUM_BLK)[:, None] >= jnp.arange(CUM_BLK)[None, :]).astype(bf16)
    lane_ids = jnp.arange(F_PAD, dtype=jnp.int32)
    sel = jnp.where(lane_ids < 3 * ATT_HEADS, lane_ids % 3, 3).reshape(1, F_PAD)
    q_scale = LOG2E / math.sqrt(HEAD_DIM)
    blk_ids = jnp.arange(ATT_BLK)
    mask = jnp.where(blk_ids[:, None] > blk_ids[None, :], NEG_BIG, 0.0).astype(f32)

    x2 = x.reshape(b * s, d)
    for l in range(depth):
        wl = w_in[l]
        wq, wk, wv = (wl[:, :o1].astype(bf16), wl[:, o1:o2].astype(bf16), wl[:, o2:o3].astype(bf16))
        wf = jnp.pad(jnp.repeat(wl[:, o3:o4], 3, axis=1), ((0, 0), (0, F_PAD - 3 * ATT_HEADS))).astype(bf16)
        wa, wg = wl[:, o4:o5].astype(bf16), wl[:, o5:].astype(bf16)
        bfp = jnp.pad(jnp.repeat(b_f[l], 3), (0, F_PAD - 3 * ATT_HEADS)).reshape(1, F_PAD)
        gq = jnp.broadcast_to((jnp.tile(q_norm_g[l], ATT_HEADS) * q_scale)[:, None], (ATT_WIDTH, LANES))
        gk = jnp.broadcast_to(jnp.tile(k_norm_g[l], ATT_HEADS)[:, None], (ATT_WIDTH, LANES))

        qt3, k3, vt3, h3, lf3 = _in_proj(x2.reshape(b, s, d), norm1_g[l].reshape(1, d), wq, wk, wv, wa, wg, wf,
                                        bfp, gq, gk)
        cp3 = _cumsum(lf3, tri, sel)
        att = _attention(qt3, k3, cp3, vt3, mask)
        cw8 = jnp.broadcast_to(conv_w[l][:, None, :], (CONV_WIDTH, SUBLANES, CONV_CH))
        wo = w_o[l].astype(bf16)
        x2 = _mix_mlp(x2, att.reshape(b * s, ATT_WIDTH), h3.reshape(b * s, CONV_CH), cw8,
                      conv_b[l].reshape(1, CONV_CH), conv_ln_g[l].reshape(1, CONV_CH),
                      conv_ln_b[l].reshape(1, CONV_CH), wo[:ATT_WIDTH], wo[ATT_WIDTH:],
                      norm2_g[l].reshape(1, d), w_mlp_in[l].astype(bf16), w_mlp_out[l].astype(bf16), s)
    return x2.reshape(b, s, d)
```

```python
import functools
import math

import jax
import jax.numpy as jnp
from jax import lax
from jax.experimental import pallas as pl
from jax.experimental.pallas import tpu as pltpu

D_MODEL = 1024
ATT_HEADS = 8
HEAD_DIM = 64
ATT_WIDTH = ATT_HEADS * HEAD_DIM
CONV_CH = D_MODEL - ATT_WIDTH
CONV_WIDTH = 31
D_FF = 4 * D_MODEL
EPS = 1e-6
LOG2E = 1.4426950408889634

LANES = 128
SUBLANES = 8
F_PAD = LANES
HALO = 32
NEG_BIG = -1e30

TM_IN = 512
TM_MIX = 512
FF_CHUNK = 512
ATT_BLK = 512
ACC_ROWS = HEAD_DIM + 16
CONV_ROWS = 32
CUM_BLK = 256
VMEM_LIMIT = 56 * 1024 * 1024

bf16 = jnp.bfloat16
f32 = jnp.float32


def _dot(a, b):
    return jnp.dot(a, b, preferred_element_type=f32)


def _split3(y):
    hi = y.astype(bf16)
    r = y - hi.astype(f32)
    mid = r.astype(bf16)
    lo = (r - mid.astype(f32)).astype(bf16)
    return hi, mid, lo


def _in_proj_kernel(x_ref, g1_ref, wq_ref, wk_ref, wv_ref, wa_ref, wg_ref, wf_ref, bf_ref,
                    gq_ref, gk_ref, qt_ref, k_ref, vt_ref, h_ref, lf_ref):
    x = x_ref[0]
    ms = jnp.mean(x * x, axis=-1, keepdims=True)
    u = (x * lax.rsqrt(ms + EPS) * g1_ref[...]).astype(bf16)

    def head_norm_t(y, gain_ref):
        yt = y.T
        out = []
        for h in range(ATT_HEADS):
            yh = yt[h * HEAD_DIM:(h + 1) * HEAD_DIM, :]
            ss = jnp.sum(yh * yh, axis=0, keepdims=True)
            out.append(yh * lax.rsqrt(ss * (1.0 / HEAD_DIM) + EPS))
        gain = jnp.concatenate([gain_ref[...]] * (TM_IN // LANES), axis=1)
        return jnp.concatenate(out, axis=0) * gain

    qt_ref[0] = head_norm_t(_dot(u, wq_ref[...]), gq_ref).astype(bf16)
    k_ref[0] = head_norm_t(_dot(u, wk_ref[...]), gk_ref).T.astype(bf16)
    vt_ref[0] = _dot(u, wv_ref[...]).T.astype(bf16)
    a = _dot(u, wa_ref[...])
    g = _dot(u, wg_ref[...])
    h_ref[0] = a * jax.nn.sigmoid(g)
    lf_ref[0] = jax.nn.log_sigmoid(_dot(u, wf_ref[...]) + bf_ref[...])


def _in_proj(x3, g1, wq, wk, wv, wa, wg, wf, bfp, gq, gk, layer):
    b, s, _ = x3.shape
    row = lambda n: pl.BlockSpec((1, TM_IN, n), lambda bi, i: (bi, i, 0))
    col = pl.BlockSpec((1, ATT_WIDTH, TM_IN), lambda bi, i: (bi, 0, i))
    full = lambda a: pl.BlockSpec(a.shape, lambda bi, i: (0,) * a.ndim)
    slab = lambda a: pl.BlockSpec((None,) + a.shape[1:], lambda bi, i: (layer, 0, 0))
    return pl.pallas_call(
        _in_proj_kernel,
        grid=(b, s // TM_IN),
        in_specs=[row(D_MODEL), full(g1)] + [slab(a) for a in (wq, wk, wv, wa, wg, wf)]
        + [full(a) for a in (bfp, gq, gk)],
        out_specs=[col, row(ATT_WIDTH), col, row(CONV_CH), row(F_PAD)],
        out_shape=[jax.ShapeDtypeStruct((b, ATT_WIDTH, s), bf16), jax.ShapeDtypeStruct((b, s, ATT_WIDTH), bf16),
                   jax.ShapeDtypeStruct((b, ATT_WIDTH, s), bf16),
                   jax.ShapeDtypeStruct((b, s, CONV_CH), f32), jax.ShapeDtypeStruct((b, s, F_PAD), f32)],
        compiler_params=pltpu.CompilerParams(dimension_semantics=("parallel", "parallel"),
                                             vmem_limit_bytes=VMEM_LIMIT),
        name="in_proj",
    )(x3, g1, wq, wk, wv, wa, wg, wf, bfp, gq, gk)


def _cumsum_kernel(lf_ref, tri_ref, sel_ref, cp_ref):
    s = lf_ref.shape[1]
    tri = tri_ref[...]
    sel = jnp.broadcast_to(sel_ref[...], (CUM_BLK, F_PAD))
    carry = jnp.zeros((1, F_PAD), f32)
    for j in range(s // CUM_BLK):
        rows = slice(j * CUM_BLK, (j + 1) * CUM_BLK)
        hi, mid, lo = _split3(lf_ref[0, rows, :])
        cs = _dot(tri, hi) + _dot(tri, mid) + _dot(tri, lo) + carry
        carry = cs[CUM_BLK - 1:CUM_BLK, :]
        bh, bm, bl = _split3(cs * (-LOG2E))
        piece = jnp.where(sel == 0, bh.astype(f32), jnp.where(sel == 1, bm.astype(f32), bl.astype(f32)))
        cp_ref[0, rows, :] = jnp.where(sel < 3, piece, 0.0).astype(bf16)


def _cumsum(lf3, tri, sel):
    b, s, _ = lf3.shape
    return pl.pallas_call(
        _cumsum_kernel,
        grid=(b,),
        in_specs=[pl.BlockSpec((1, s, F_PAD), lambda i: (i, 0, 0)),
                  pl.BlockSpec(tri.shape, lambda i: (0, 0)),
                  pl.BlockSpec(sel.shape, lambda i: (0, 0))],
        out_specs=pl.BlockSpec((1, s, F_PAD), lambda i: (i, 0, 0)),
        out_shape=jax.ShapeDtypeStruct((b, s, F_PAD), bf16),
        compiler_params=pltpu.CompilerParams(dimension_semantics=("parallel",)),
        name="forget_cumsum",
    )(lf3, tri, sel)


def _attn_kernel(qt_ref, k_ref, cp_ref, vt_ref, mask_ref, slot_ref, o_ref, acc_ref, st_a, st_b):
    hp = pl.program_id(1)
    nblk = k_ref.shape[1] // ATT_BLK
    rows = [slice(HEAD_DIM * h, HEAD_DIM * (h + 1)) for h in range(2)]
    row_id = lax.broadcasted_iota(jnp.int32, (LANES, ATT_BLK), 0)
    zeros_t = jnp.zeros((HEAD_DIM, ATT_BLK), bf16)
    ones_t = [jnp.where((row_id >= 3 * (2 * hp + h)) & (row_id < 3 * (2 * hp + h) + 3), 1.0, 0.0).astype(bf16)
              for h in range(2)]

    def blk(i):
        return slice(i * ATT_BLK, (i + 1) * ATT_BLK)

    def scores(qi, kj, st_ref):
        kc = jnp.concatenate([k_ref[0, blk(kj), :], cp_ref[0, blk(kj), :]], axis=1)
        qt = qt_ref[0, :, blk(qi)]
        mx = []
        for h in range(2):
            parts = [qt[rows[0], :], zeros_t] if h == 0 else [zeros_t, qt[rows[1], :]]
            st = _dot(kc, jnp.concatenate(parts + [ones_t[h]], axis=0))
            if qi == kj:
                st = st + mask_ref[...]
            st_ref[h, slot] = st
            mx.append(jnp.max(st, axis=0, keepdims=True))
        return mx

    ones_l = jnp.ones((ACC_ROWS - HEAD_DIM, ATT_BLK), bf16)

    def fold(pair, alpha, pv):
        qi, kj = pair
        for h in range(2):
            acc_ref[h] = pv[h] if kj == 0 else alpha[h] * acc_ref[h] + pv[h]
        if kj == qi:
            o_t = [acc_ref[h, 0:HEAD_DIM, :] * (1.0 / acc_ref[h, HEAD_DIM:HEAD_DIM + 1, :]) for h in range(2)]
            o_ref[0, blk(qi), :] = jnp.concatenate(o_t, axis=0).T.astype(o_ref.dtype)

    pairs = [(qi, kj) for qi in range(nblk) for kj in range(qi + 1)]
    st_bufs = (st_a, st_b)
    slot = slot_ref[0]
    mx = scores(*pairs[0], st_bufs[0])
    m_run = [None, None]
    pending = None
    for t, (qi, kj) in enumerate(pairs):
        mx_next = scores(*pairs[t + 1], st_bufs[(t + 1) % 2]) if t + 1 < len(pairs) else None
        alpha, p = [None, None], [None, None]
        for h in range(2):
            if kj == 0:
                m_new = mx[h]
            else:
                m_new = jnp.maximum(m_run[h], mx[h])
                alpha[h] = jnp.exp2(m_run[h] - m_new)
            p[h] = jnp.exp2(st_bufs[t % 2][h, slot] - m_new).astype(bf16)
            m_run[h] = m_new
        if pending is not None:
            fold(*pending)
        pv = [_dot(jnp.concatenate([vt_ref[0, rows[h], blk(kj)], ones_l], axis=0), p[h]) for h in range(2)]
        pending = ((qi, kj), alpha, pv)
        mx = mx_next
    fold(*pending)


def _attention(qt3, k3, cp3, vt3, mask):
    b, s, _ = k3.shape
    hp = ATT_WIDTH // LANES
    return pl.pallas_call(
        _attn_kernel,
        grid=(b, hp),
        in_specs=[pl.BlockSpec((1, LANES, s), lambda bi, hi: (bi, hi, 0)),
                  pl.BlockSpec((1, s, LANES), lambda bi, hi: (bi, 0, hi)),
                  pl.BlockSpec((1, s, F_PAD), lambda bi, hi: (bi, 0, 0)),
                  pl.BlockSpec((1, LANES, s), lambda bi, hi: (bi, hi, 0)),
                  pl.BlockSpec(mask.shape, lambda bi, hi: (0, 0)),
                  pl.BlockSpec(memory_space=pltpu.SMEM)],
        out_specs=pl.BlockSpec((1, s, LANES), lambda bi, hi: (bi, 0, hi)),
        out_shape=jax.ShapeDtypeStruct((b, s, ATT_WIDTH), bf16),
        scratch_shapes=[pltpu.VMEM((2, ACC_ROWS, ATT_BLK), f32),
                        pltpu.VMEM((2, 2, ATT_BLK, ATT_BLK), f32), pltpu.VMEM((2, 2, ATT_BLK, ATT_BLK), f32)],
        compiler_params=pltpu.CompilerParams(dimension_semantics=("parallel", "parallel"),
                                             vmem_limit_bytes=VMEM_LIMIT),
        name="fox_attention",
    )(qt3, k3, cp3, vt3, mask, jnp.zeros((1,), jnp.int32))


def _conv_module(h_ref, halo, cw_ref, cb_ref, lg_ref, lb_ref, win_ref, sh_ref, hc_ref):
    n_rows = h_ref.shape[0]
    win_ref[0:HALO, :] = halo
    win_ref[HALO:HALO + n_rows, :] = h_ref[...]
    n_sh = sh_ref.shape[1]
    for r in range(1, SUBLANES):
        sh_ref[r] = win_ref[r:r + n_sh, :]
    off = HALO - (CONV_WIDTH - 1)
    for c in range(n_rows // CONV_ROWS):
        acc = None
        for w in range(CONV_WIDTH):
            r = (off + w) % SUBLANES
            base = c * CONV_ROWS + (off + w) - r
            src = win_ref if r == 0 else sh_ref.at[r]
            x = src[base:base + CONV_ROWS, :].reshape(CONV_ROWS // SUBLANES, SUBLANES, -1)
            term = (x * cw_ref[w]).reshape(CONV_ROWS, -1)
            acc = term if acc is None else acc + term
        y = acc + cb_ref[...]
        mu = jnp.mean(y, axis=-1, keepdims=True)
        d = y - mu
        var = jnp.mean(d * d, axis=-1, keepdims=True)
        z = d * lax.rsqrt(var + EPS) * lg_ref[...] + lb_ref[...]
        hc_ref[c * CONV_ROWS:(c + 1) * CONV_ROWS, :] = (z * jax.nn.sigmoid(z)).astype(hc_ref.dtype)


def _mix_mlp_kernel(x_ref, att_ref, h_ref, hh_ref, cw_ref, cb_ref, lg_ref, lb_ref, woa_ref, woc_ref,
                    g2_ref, w1_ref, w2_ref, o_ref, hc_ref, win_ref, sh_ref, acc_ref, *, tiles_per_seq):
    i = pl.program_id(0)
    halo = hh_ref[...]
    halo = jnp.where(i % tiles_per_seq == 0, jnp.zeros_like(halo), halo)
    _conv_module(h_ref, halo, cw_ref, cb_ref, lg_ref, lb_ref, win_ref, sh_ref, hc_ref)

    o_ref[...] = x_ref[...] + _dot(att_ref[...], woa_ref[...]) + _dot(hc_ref[...], woc_ref[...])
    x1 = o_ref[...]
    ms = jnp.mean(x1 * x1, axis=-1, keepdims=True)
    u = (x1 * lax.rsqrt(ms + EPS) * g2_ref[...]).astype(bf16)
    for c in range(D_FF // FF_CHUNK):
        sl = slice(c * FF_CHUNK, (c + 1) * FF_CHUNK)
        h = jnp.maximum(_dot(u, w1_ref[:, sl]), 0.0)
        part = _dot((h * h).astype(bf16), w2_ref[sl, :])
        if c == 0:
            acc_ref[...] = part
        else:
            acc_ref[...] += part
    o_ref[...] += acc_ref[...]


def _mix_mlp(x2, att2, h2, cw8, cb, lg, lb, wo, g2, w1, w2, seq_len, layer):
    t = x2.shape[0]
    ratio = TM_MIX // HALO
    row = lambda w: pl.BlockSpec((TM_MIX, w), lambda i: (i, 0))
    const = lambda a: pl.BlockSpec(a.shape, lambda i: (0,) * a.ndim, pipeline_mode=pl.Buffered(1))
    slab = lambda a, rows=None, part=0: pl.BlockSpec((None, rows or a.shape[1], a.shape[2]),
                                                     lambda i: (layer, part, 0), pipeline_mode=pl.Buffered(1))
    return pl.pallas_call(
        functools.partial(_mix_mlp_kernel, tiles_per_seq=seq_len // TM_MIX),
        grid=(t // TM_MIX,),
        in_specs=[row(D_MODEL), row(ATT_WIDTH), row(CONV_CH),
                  pl.BlockSpec((HALO, CONV_CH), lambda i: (jnp.maximum(i * ratio - 1, 0), 0)),
                  const(cw8), const(cb), const(lg), const(lb),
                  slab(wo, ATT_WIDTH, 0), slab(wo, CONV_CH, 1), const(g2), slab(w1), slab(w2)],
        out_specs=row(D_MODEL),
        out_shape=jax.ShapeDtypeStruct((t, D_MODEL), f32),
        scratch_shapes=[pltpu.VMEM((TM_MIX, CONV_CH), bf16),
                        pltpu.VMEM((HALO + TM_MIX, CONV_CH), f32),
                        pltpu.VMEM((SUBLANES, HALO + TM_MIX - SUBLANES, CONV_CH), f32),
                        pltpu.VMEM((TM_MIX, D_MODEL), f32)],
        compiler_params=pltpu.CompilerParams(dimension_semantics=("parallel",), vmem_limit_bytes=VMEM_LIMIT),
        name="mix_mlp",
    )(x2, att2, h2, h2, cw8, cb, lg, lb, wo, wo, g2, w1, w2)


def kernel(x, norm1_g, w_in, b_f, q_norm_g, k_norm_g, conv_w, conv_b, conv_ln_g, conv_ln_b, w_o,
           norm2_g, w_mlp_in, w_mlp_out):
    b, s, d = x.shape
    depth = w_in.shape[0]
    o1, o2, o3 = ATT_WIDTH, 2 * ATT_WIDTH, 3 * ATT_WIDTH
    o4 = o3 + ATT_HEADS
    o5 = o4 + CONV_CH

    tri = (jnp.arange(CUM_BLK)[:, None] >= jnp.arange(CUM_BLK)[None, :]).astype(bf16)
    lane_ids = jnp.arange(F_PAD, dtype=jnp.int32)
    sel = jnp.where(lane_ids < 3 * ATT_HEADS, lane_ids % 3, 3).reshape(1, F_PAD)
    q_scale = LOG2E / math.sqrt(HEAD_DIM)
    blk_ids = jnp.arange(ATT_BLK)
    mask = jnp.where(blk_ids[:, None] > blk_ids[None, :], NEG_BIG, 0.0).astype(f32)

    wq, wk, wv = (w_in[:, :, :o1].astype(bf16), w_in[:, :, o1:o2].astype(bf16), w_in[:, :, o2:o3].astype(bf16))
    wf = jnp.pad(jnp.repeat(w_in[:, :, o3:o4], 3, axis=2), ((0, 0), (0, 0), (0, F_PAD - 3 * ATT_HEADS))).astype(bf16)
    wa, wg = w_in[:, :, o4:o5].astype(bf16), w_in[:, :, o5:].astype(bf16)
    wo, w1, w2 = w_o.astype(bf16), w_mlp_in.astype(bf16), w_mlp_out.astype(bf16)

    x2 = x.reshape(b * s, d)
    for l in range(depth):
        bfp = jnp.pad(jnp.repeat(b_f[l], 3), (0, F_PAD - 3 * ATT_HEADS)).reshape(1, F_PAD)
        gq = jnp.broadcast_to((jnp.tile(q_norm_g[l], ATT_HEADS) * q_scale)[:, None], (ATT_WIDTH, LANES))
        gk = jnp.broadcast_to(jnp.tile(k_norm_g[l], ATT_HEADS)[:, None], (ATT_WIDTH, LANES))

        qt3, k3, vt3, h3, lf3 = _in_proj(x2.reshape(b, s, d), norm1_g[l].reshape(1, d), wq, wk, wv, wa, wg, wf,
                                        bfp, gq, gk, l)
        cp3 = _cumsum(lf3, tri, sel)
        att = _attention(qt3, k3, cp3, vt3, mask)
        cw8 = jnp.broadcast_to(conv_w[l][:, None, :], (CONV_WIDTH, SUBLANES, CONV_CH))
        x2 = _mix_mlp(x2, att.reshape(b * s, ATT_WIDTH), h3.reshape(b * s, CONV_CH), cw8,
                      conv_b[l].reshape(1, CONV_CH), conv_ln_g[l].reshape(1, CONV_CH),
                      conv_ln_b[l].reshape(1, CONV_CH), wo, norm2_g[l].reshape(1, d), w1, w2, s, l)
    return x2.reshape(b, s, d)
```

```python
import functools
import math

import jax
import jax.numpy as jnp
from jax import lax
from jax.experimental import pallas as pl
from jax.experimental.pallas import tpu as pltpu

D_MODEL = 1024
ATT_HEADS = 8
HEAD_DIM = 64
ATT_WIDTH = ATT_HEADS * HEAD_DIM
CONV_CH = D_MODEL - ATT_WIDTH
CONV_WIDTH = 31
D_FF = 4 * D_MODEL
EPS = 1e-6
LOG2E = 1.4426950408889634

LANES = 128
SUBLANES = 8
F_PAD = LANES
HALO = 32
NEG_BIG = -1e30

TM_IN = 512
TM_MIX = 512
FF_CHUNK = 512
ATT_BLK = 512
HALF = ATT_BLK // 2
ACC_ROWS = HEAD_DIM + 16
CONV_ROWS = 32
CUM_BLK = 256
VMEM_LIMIT = 56 * 1024 * 1024

bf16 = jnp.bfloat16
f32 = jnp.float32


def _dot(a, b):
    return jnp.dot(a, b, preferred_element_type=f32)


def _split3(y):
    hi = y.astype(bf16)
    r = y - hi.astype(f32)
    mid = r.astype(bf16)
    lo = (r - mid.astype(f32)).astype(bf16)
    return hi, mid, lo


def _in_proj_kernel(x_ref, g1_ref, wq_ref, wk_ref, wv_ref, wa_ref, wg_ref, wf_ref, bf_ref,
                    gq_ref, gk_ref, qt_ref, k_ref, vt_ref, h_ref, lf_ref):
    x = x_ref[0]
    ms = jnp.mean(x * x, axis=-1, keepdims=True)
    u = (x * lax.rsqrt(ms + EPS) * g1_ref[...]).astype(bf16)

    def head_norm_t(y, gain_ref):
        yt = y.T
        out = []
        for h in range(ATT_HEADS):
            yh = yt[h * HEAD_DIM:(h + 1) * HEAD_DIM, :]
            ss = jnp.sum(yh * yh, axis=0, keepdims=True)
            out.append(yh * lax.rsqrt(ss * (1.0 / HEAD_DIM) + EPS))
        gain = jnp.concatenate([gain_ref[...]] * (TM_IN // LANES), axis=1)
        return jnp.concatenate(out, axis=0) * gain

    qt_ref[0] = head_norm_t(_dot(u, wq_ref[...]), gq_ref).astype(bf16)
    k_ref[0] = head_norm_t(_dot(u, wk_ref[...]), gk_ref).T.astype(bf16)
    vt_ref[0] = _dot(u, wv_ref[...]).T.astype(bf16)
    a = _dot(u, wa_ref[...])
    g = _dot(u, wg_ref[...])
    h_ref[0] = a * jax.nn.sigmoid(g)
    lf_ref[0] = jax.nn.log_sigmoid(_dot(u, wf_ref[...]) + bf_ref[...])


def _in_proj(x3, g1, wq, wk, wv, wa, wg, wf, bfp, gq, gk, layer):
    b, s, _ = x3.shape
    row = lambda n: pl.BlockSpec((1, TM_IN, n), lambda bi, i: (bi, i, 0))
    col = pl.BlockSpec((1, ATT_WIDTH, TM_IN), lambda bi, i: (bi, 0, i))
    full = lambda a: pl.BlockSpec(a.shape, lambda bi, i: (0,) * a.ndim)
    slab = lambda a: pl.BlockSpec((None,) + a.shape[1:], lambda bi, i: (layer, 0, 0))
    return pl.pallas_call(
        _in_proj_kernel,
        grid=(b, s // TM_IN),
        in_specs=[row(D_MODEL), full(g1)] + [slab(a) for a in (wq, wk, wv, wa, wg, wf)]
        + [full(a) for a in (bfp, gq, gk)],
        out_specs=[col, row(ATT_WIDTH), col, row(CONV_CH), row(F_PAD)],
        out_shape=[jax.ShapeDtypeStruct((b, ATT_WIDTH, s), bf16), jax.ShapeDtypeStruct((b, s, ATT_WIDTH), bf16),
                   jax.ShapeDtypeStruct((b, ATT_WIDTH, s), bf16),
                   jax.ShapeDtypeStruct((b, s, CONV_CH), f32), jax.ShapeDtypeStruct((b, s, F_PAD), f32)],
        compiler_params=pltpu.CompilerParams(dimension_semantics=("parallel", "parallel"),
                                             vmem_limit_bytes=VMEM_LIMIT),
        name="in_proj",
    )(x3, g1, wq, wk, wv, wa, wg, wf, bfp, gq, gk)


def _cumsum_kernel(lf_ref, tri_ref, sel_ref, cp_ref):
    s = lf_ref.shape[1]
    tri = tri_ref[...]
    sel = jnp.broadcast_to(sel_ref[...], (CUM_BLK, F_PAD))
    carry = jnp.zeros((1, F_PAD), f32)
    for j in range(s // CUM_BLK):
        rows = slice(j * CUM_BLK, (j + 1) * CUM_BLK)
        hi, mid, lo = _split3(lf_ref[0, rows, :])
        cs = _dot(tri, hi) + _dot(tri, mid) + _dot(tri, lo) + carry
        carry = cs[CUM_BLK - 1:CUM_BLK, :]
        bh, bm, bl = _split3(cs * (-LOG2E))
        piece = jnp.where(sel == 0, bh.astype(f32), jnp.where(sel == 1, bm.astype(f32), bl.astype(f32)))
        cp_ref[0, rows, :] = jnp.where(sel < 3, piece, 0.0).astype(bf16)


def _cumsum(lf3, tri, sel):
    b, s, _ = lf3.shape
    return pl.pallas_call(
        _cumsum_kernel,
        grid=(b,),
        in_specs=[pl.BlockSpec((1, s, F_PAD), lambda i: (i, 0, 0)),
                  pl.BlockSpec(tri.shape, lambda i: (0, 0)),
                  pl.BlockSpec(sel.shape, lambda i: (0, 0))],
        out_specs=pl.BlockSpec((1, s, F_PAD), lambda i: (i, 0, 0)),
        out_shape=jax.ShapeDtypeStruct((b, s, F_PAD), bf16),
        compiler_params=pltpu.CompilerParams(dimension_semantics=("parallel",)),
        name="forget_cumsum",
    )(lf3, tri, sel)


def _attn_kernel(qt_ref, k_ref, cp_ref, vt_ref, mask_ref, slot_ref, o_ref, acc_ref, st_a, st_b):
    hp = pl.program_id(1)
    nblk = k_ref.shape[1] // ATT_BLK
    rows = [slice(HEAD_DIM * h, HEAD_DIM * (h + 1)) for h in range(2)]
    row_id = lax.broadcasted_iota(jnp.int32, (LANES, ATT_BLK), 0)
    zeros_t = jnp.zeros((HEAD_DIM, ATT_BLK), bf16)
    ones_t = [jnp.where((row_id >= 3 * (2 * hp + h)) & (row_id < 3 * (2 * hp + h) + 3), 1.0, 0.0).astype(bf16)
              for h in range(2)]

    def blk(i):
        return slice(i * ATT_BLK, (i + 1) * ATT_BLK)

    def scores(qi, kj, st_ref):
        kc = jnp.concatenate([k_ref[0, blk(kj), :], cp_ref[0, blk(kj), :]], axis=1)
        qt = qt_ref[0, :, blk(qi)]
        mx = []
        for h in range(2):
            parts = [qt[rows[0], :], zeros_t] if h == 0 else [zeros_t, qt[rows[1], :]]
            w = jnp.concatenate(parts + [ones_t[h]], axis=0)
            if qi != kj:
                st = _dot(kc, w)
                st_ref[h, slot] = st
                mx.append(jnp.max(st, axis=0, keepdims=True))
            else:
                top = _dot(kc[0:HALF, :], w) + mask_ref[0:HALF, :]
                bot = _dot(kc[HALF:, :], w[:, HALF:]) + mask_ref[HALF:, HALF:]
                st_ref[h, slot, 0:HALF, :] = top
                st_ref[h, slot, HALF:, HALF:] = bot
                bot_mx = jnp.concatenate([jnp.full((1, HALF), NEG_BIG, f32),
                                          jnp.max(bot, axis=0, keepdims=True)], axis=1)
                mx.append(jnp.maximum(jnp.max(top, axis=0, keepdims=True), bot_mx))
        return mx

    ones_l = jnp.ones((ACC_ROWS - HEAD_DIM, ATT_BLK), bf16)

    def fold(pair, alpha, pv):
        qi, kj = pair
        for h in range(2):
            acc_ref[h] = pv[h] if kj == 0 else alpha[h] * acc_ref[h] + pv[h]
        if kj == qi:
            o_t = [acc_ref[h, 0:HEAD_DIM, :] * (1.0 / acc_ref[h, HEAD_DIM:HEAD_DIM + 1, :]) for h in range(2)]
            o_ref[0, blk(qi), :] = jnp.concatenate(o_t, axis=0).T.astype(o_ref.dtype)

    pairs = [(qi, kj) for qi in range(nblk) for kj in range(qi + 1)]
    st_bufs = (st_a, st_b)
    slot = slot_ref[0]
    mx = scores(*pairs[0], st_bufs[0])
    m_run = [None, None]
    pending = None
    for t, (qi, kj) in enumerate(pairs):
        mx_next = scores(*pairs[t + 1], st_bufs[(t + 1) % 2]) if t + 1 < len(pairs) else None
        alpha, p = [None, None], [None, None]
        for h in range(2):
            if kj == 0:
                m_new = mx[h]
            else:
                m_new = jnp.maximum(m_run[h], mx[h])
                alpha[h] = jnp.exp2(m_run[h] - m_new)
            st_ref = st_bufs[t % 2]
            if kj != qi:
                p[h] = jnp.exp2(st_ref[h, slot] - m_new).astype(bf16)
            else:
                p[h] = (jnp.exp2(st_ref[h, slot, 0:HALF, :] - m_new).astype(bf16),
                        jnp.exp2(st_ref[h, slot, HALF:, HALF:] - m_new[:, HALF:]).astype(bf16))
            m_run[h] = m_new
        if pending is not None:
            fold(*pending)
        pv = []
        for h in range(2):
            vtl = jnp.concatenate([vt_ref[0, rows[h], blk(kj)], ones_l], axis=0)
            if kj != qi:
                pv.append(_dot(vtl, p[h]))
            else:
                top = _dot(vtl[:, 0:HALF], p[h][0])
                bot = _dot(vtl[:, HALF:], p[h][1])
                pv.append(jnp.concatenate([top[:, 0:HALF], top[:, HALF:] + bot], axis=1))
        pending = ((qi, kj), alpha, pv)
        mx = mx_next
    fold(*pending)


def _attention(qt3, k3, cp3, vt3, mask):
    b, s, _ = k3.shape
    hp = ATT_WIDTH // LANES
    return pl.pallas_call(
        _attn_kernel,
        grid=(b, hp),
        in_specs=[pl.BlockSpec((1, LANES, s), lambda bi, hi: (bi, hi, 0)),
                  pl.BlockSpec((1, s, LANES), lambda bi, hi: (bi, 0, hi)),
                  pl.BlockSpec((1, s, F_PAD), lambda bi, hi: (bi, 0, 0)),
                  pl.BlockSpec((1, LANES, s), lambda bi, hi: (bi, hi, 0)),
                  pl.BlockSpec(mask.shape, lambda bi, hi: (0, 0)),
                  pl.BlockSpec(memory_space=pltpu.SMEM)],
        out_specs=pl.BlockSpec((1, s, LANES), lambda bi, hi: (bi, 0, hi)),
        out_shape=jax.ShapeDtypeStruct((b, s, ATT_WIDTH), bf16),
        scratch_shapes=[pltpu.VMEM((2, ACC_ROWS, ATT_BLK), f32),
                        pltpu.VMEM((2, 2, ATT_BLK, ATT_BLK), f32), pltpu.VMEM((2, 2, ATT_BLK, ATT_BLK), f32)],
        compiler_params=pltpu.CompilerParams(dimension_semantics=("parallel", "parallel"),
                                             vmem_limit_bytes=VMEM_LIMIT),
        name="fox_attention",
    )(qt3, k3, cp3, vt3, mask, jnp.zeros((1,), jnp.int32))


def _conv_module(h_ref, halo, cw_ref, cb_ref, lg_ref, lb_ref, win_ref, sh_ref, hc_ref):
    n_rows = h_ref.shape[0]
    win_ref[0:HALO, :] = halo
    win_ref[HALO:HALO + n_rows, :] = h_ref[...]
    n_sh = sh_ref.shape[1]
    for r in range(1, SUBLANES):
        sh_ref[r] = win_ref[r:r + n_sh, :]
    off = HALO - (CONV_WIDTH - 1)
    for c in range(n_rows // CONV_ROWS):
        acc = None
        for w in range(CONV_WIDTH):
            r = (off + w) % SUBLANES
            base = c * CONV_ROWS + (off + w) - r
            src = win_ref if r == 0 else sh_ref.at[r]
            x = src[base:base + CONV_ROWS, :].reshape(CONV_ROWS // SUBLANES, SUBLANES, -1)
            term = (x * cw_ref[w]).reshape(CONV_ROWS, -1)
            acc = term if acc is None else acc + term
        y = acc + cb_ref[...]
        mu = jnp.mean(y, axis=-1, keepdims=True)
        d = y - mu
        var = jnp.mean(d * d, axis=-1, keepdims=True)
        z = d * lax.rsqrt(var + EPS) * lg_ref[...] + lb_ref[...]
        hc_ref[c * CONV_ROWS:(c + 1) * CONV_ROWS, :] = (z * jax.nn.sigmoid(z)).astype(hc_ref.dtype)


def _mix_mlp_kernel(x_ref, att_ref, h_ref, hh_ref, cw_ref, cb_ref, lg_ref, lb_ref, woa_ref, woc_ref,
                    g2_ref, w1_ref, w2_ref, o_ref, hc_ref, win_ref, sh_ref, acc_ref, *, tiles_per_seq):
    i = pl.program_id(0)
    halo = hh_ref[...]
    halo = jnp.where(i % tiles_per_seq == 0, jnp.zeros_like(halo), halo)
    _conv_module(h_ref, halo, cw_ref, cb_ref, lg_ref, lb_ref, win_ref, sh_ref, hc_ref)

    o_ref[...] = x_ref[...] + _dot(att_ref[...], woa_ref[...]) + _dot(hc_ref[...], woc_ref[...])
    x1 = o_ref[...]
    ms = jnp.mean(x1 * x1, axis=-1, keepdims=True)
    u = (x1 * lax.rsqrt(ms + EPS) * g2_ref[...]).astype(bf16)
    for c in range(D_FF // FF_CHUNK):
        sl = slice(c * FF_CHUNK, (c + 1) * FF_CHUNK)
        h = jnp.maximum(_dot(u, w1_ref[:, sl]), 0.0)
        part = _dot((h * h).astype(bf16), w2_ref[sl, :])
        if c == 0:
            acc_ref[...] = part
        else:
            acc_ref[...] += part
    o_ref[...] += acc_ref[...]


def _mix_mlp(x2, att2, h2, cw8, cb, lg, lb, wo, g2, w1, w2, seq_len, layer):
    t = x2.shape[0]
    ratio = TM_MIX // HALO
    row = lambda w: pl.BlockSpec((TM_MIX, w), lambda i: (i, 0))
    const = lambda a: pl.BlockSpec(a.shape, lambda i: (0,) * a.ndim, pipeline_mode=pl.Buffered(1))
    slab = lambda a, rows=None, part=0: pl.BlockSpec((None, rows or a.shape[1], a.shape[2]),
                                                     lambda i: (layer, part, 0), pipeline_mode=pl.Buffered(1))
    return pl.pallas_call(
        functools.partial(_mix_mlp_kernel, tiles_per_seq=seq_len // TM_MIX),
        grid=(t // TM_MIX,),
        in_specs=[row(D_MODEL), row(ATT_WIDTH), row(CONV_CH),
                  pl.BlockSpec((HALO, CONV_CH), lambda i: (jnp.maximum(i * ratio - 1, 0), 0)),
                  const(cw8), const(cb), const(lg), const(lb),
                  slab(wo, ATT_WIDTH, 0), slab(wo, CONV_CH, 1), const(g2), slab(w1), slab(w2)],
        out_specs=row(D_MODEL),
        out_shape=jax.ShapeDtypeStruct((t, D_MODEL), f32),
        scratch_shapes=[pltpu.VMEM((TM_MIX, CONV_CH), bf16),
                        pltpu.VMEM((HALO + TM_MIX, CONV_CH), f32),
                        pltpu.VMEM((SUBLANES, HALO + TM_MIX - SUBLANES, CONV_CH), f32),
                        pltpu.VMEM((TM_MIX, D_MODEL), f32)],
        compiler_params=pltpu.CompilerParams(dimension_semantics=("parallel",), vmem_limit_bytes=VMEM_LIMIT),
        name="mix_mlp",
    )(x2, att2, h2, h2, cw8, cb, lg, lb, wo, wo, g2, w1, w2)


def kernel(x, norm1_g, w_in, b_f, q_norm_g, k_norm_g, conv_w, conv_b, conv_ln_g, conv_ln_b, w_o,
           norm2_g, w_mlp_in, w_mlp_out):
    b, s, d = x.shape
    depth = w_in.shape[0]
    o1, o2, o3 = ATT_WIDTH, 2 * ATT_WIDTH, 3 * ATT_WIDTH
    o4 = o3 + ATT_HEADS
    o5 = o4 + CONV_CH

    tri = (jnp.arange(CUM_BLK)[:, None] >= jnp.arange(CUM_BLK)[None, :]).astype(bf16)
    lane_ids = jnp.arange(F_PAD, dtype=jnp.int32)
    sel = jnp.where(lane_ids < 3 * ATT_HEADS, lane_ids % 3, 3).reshape(1, F_PAD)
    q_scale = LOG2E / math.sqrt(HEAD_DIM)
    blk_ids = jnp.arange(ATT_BLK)
    mask = jnp.where(blk_ids[:, None] > blk_ids[None, :], NEG_BIG, 0.0).astype(f32)

    wq, wk, wv = (w_in[:, :, :o1].astype(bf16), w_in[:, :, o1:o2].astype(bf16), w_in[:, :, o2:o3].astype(bf16))
    wf = jnp.pad(jnp.repeat(w_in[:, :, o3:o4], 3, axis=2), ((0, 0), (0, 0), (0, F_PAD - 3 * ATT_HEADS))).astype(bf16)
    wa, wg = w_in[:, :, o4:o5].astype(bf16), w_in[:, :, o5:].astype(bf16)
    wo, w1, w2 = w_o.astype(bf16), w_mlp_in.astype(bf16), w_mlp_out.astype(bf16)

    x2 = x.reshape(b * s, d)
    for l in range(depth):
        bfp = jnp.pad(jnp.repeat(b_f[l], 3), (0, F_PAD - 3 * ATT_HEADS)).reshape(1, F_PAD)
        gq = jnp.broadcast_to((jnp.tile(q_norm_g[l], ATT_HEADS) * q_scale)[:, None], (ATT_WIDTH, LANES))
        gk = jnp.broadcast_to(jnp.tile(k_norm_g[l], ATT_HEADS)[:, None], (ATT_WIDTH, LANES))

        qt3, k3, vt3, h3, lf3 = _in_proj(x2.reshape(b, s, d), norm1_g[l].reshape(1, d), wq, wk, wv, wa, wg, wf,
                                        bfp, gq, gk, l)
        cp3 = _cumsum(lf3, tri, sel)
        att = _attention(qt3, k3, cp3, vt3, mask)
        cw8 = jnp.broadcast_to(conv_w[l][:, None, :], (CONV_WIDTH, SUBLANES, CONV_CH))
        x2 = _mix_mlp(x2, att.reshape(b * s, ATT_WIDTH), h3.reshape(b * s, CONV_CH), cw8,
                      conv_b[l].reshape(1, CONV_CH), conv_ln_g[l].reshape(1, CONV_CH),
                      conv_ln_b[l].reshape(1, CONV_CH), wo, norm2_g[l].reshape(1, d), w1, w2, s, l)
    return x2.reshape(b, s, d)
```
